```python
import math
import jax, jax.numpy as jnp
from jax import lax
import numpy as np

D_MODEL = 1024
BATCH = 4
SEQ = 4096
DEPTH = 4

N_MEM = 256
XA_HEADS = 4
XA_HEAD_DIM = D_MODEL // XA_HEADS
HEAD_DIM = 64
DIL_GROUPS = ((128, 1), (512, 4), (2048, 16))
N_DIL = len(DIL_GROUPS)
HEADS_PER_GROUP = 8
ATT_WIDTH = N_DIL * HEADS_PER_GROUP * HEAD_DIM
ATT_OUT = HEADS_PER_GROUP * HEAD_DIM
SSD_EXPAND = 2
SSD_INNER = SSD_EXPAND * D_MODEL
SSD_HEAD_DIM = 64
SSD_HEADS = SSD_INNER // SSD_HEAD_DIM
SSD_STATE = 128
SSD_GROUPS = 8
SSD_CONV = 4
SSD_CHUNK = 128
SSD_CONV_CH = SSD_INNER + 2 * SSD_GROUPS * SSD_STATE
N_BRANCH = 2
IN_WIDTH = 3 * ATT_WIDTH + SSD_INNER + SSD_CONV_CH + SSD_HEADS + N_BRANCH * D_MODEL
D_FF = 2816
N_SUBNORMS = 8
EPS = 1e-6

kernel_name = "hybrid_dilated_attn_ssd_macaron_trunk"


def rmsnorm(x, g):
    xf = x.astype(jnp.float32)
    y = xf * lax.rsqrt(jnp.mean(xf * xf, axis=-1, keepdims=True) + EPS)
    return (y * g.astype(jnp.float32)).astype(x.dtype)


def swiglu(x, w_gate, w_up, w_down):
    return (jax.nn.silu(x @ w_gate) * (x @ w_up)) @ w_down


def dilated_group_attention(q, k, v, window, dilation):
    b, s, h, hd = q.shape
    band = window // dilation
    blk = band
    L = s // dilation
    nb = -(-L // blk)
    Lp = nb * blk

    def to_sub(t):
        t = t.reshape(b, L, dilation, h, hd).transpose(0, 2, 3, 1, 4)
        return jnp.pad(t, ((0, 0), (0, 0), (0, 0), (0, Lp - L), (0, 0)))

    def band_keys(t):
        tb = t.reshape(b, dilation, h, nb, blk, hd)
        prev = jnp.pad(tb, ((0, 0), (0, 0), (0, 0), (1, 0), (0, 0), (0, 0)))[:, :, :, :nb]
        return jnp.concatenate([prev, tb], axis=4)

    qb = to_sub(q).reshape(b, dilation, h, nb, blk, hd)
    kb = band_keys(to_sub(k))
    vb = band_keys(to_sub(v))
    scores = jnp.einsum('brhnqd,brhnkd->brhnqk', qb, kb).astype(jnp.float32) * (hd ** -0.5)
    qpos = jnp.arange(nb)[:, None, None] * blk + jnp.arange(blk)[None, :, None]
    kpos = jnp.arange(nb)[:, None, None] * blk - blk + jnp.arange(2 * blk)[None, None, :]
    diff = qpos - kpos
    mask = (diff >= 0) & (diff <= band) & (kpos >= 0)
    scores = jnp.where(mask, scores, -jnp.inf)
    m = jnp.max(scores, axis=-1, keepdims=True)
    p = jnp.exp(scores - m)
    denom = jnp.sum(p, axis=-1, keepdims=True)
    out = jnp.einsum('brhnqk,brhnkd->brhnqd', (p / denom).astype(v.dtype), vb)
    lse = (m + jnp.log(denom))[..., 0]
    out = out.reshape(b, dilation, h, Lp, hd)[:, :, :, :L].transpose(0, 3, 1, 2, 4).reshape(b, s, h, hd)
    lse = lse.reshape(b, dilation, h, Lp)[:, :, :, :L].transpose(0, 3, 1, 2).reshape(b, s, h)
    return out, lse


def dilated_attention(q, k, v):
    outs, lses = [], []
    for g, (w, d) in enumerate(DIL_GROUPS):
        o, l = dilated_group_attention(q[:, :, g], k[:, :, g], v[:, :, g], w, d)
        outs.append(o)
        lses.append(l)
    o = jnp.stack(outs, axis=2)
    alpha = jax.nn.softmax(jnp.stack(lses, axis=2), axis=2)
    return jnp.einsum('bsgh,bsghd->bshd', alpha.astype(o.dtype), o)


def causal_depthwise_conv(x, w, bias):
    K, C = w.shape
    y = lax.conv_general_dilated(x, w[:, None, :].astype(x.dtype), window_strides=(1,),
                                 padding=[(K - 1, 0)], dimension_numbers=('NWC', 'WIO', 'NWC'),
                                 feature_group_count=C)
    return y + bias


def ssd_chunked(x, dt, A, Bm, Cm):
    b, s, nh, P = x.shape
    G, N = Bm.shape[2], Bm.shape[3]
    R = nh // G
    Q = SSD_CHUNK
    c = s // Q
    xdt = (x * dt[..., None]).reshape(b, c, Q, G, R, P)
    a = (dt * A).reshape(b, c, Q, G, R).transpose(0, 1, 3, 4, 2)
    Bc = Bm.reshape(b, c, Q, G, N)
    Cc = Cm.reshape(b, c, Q, G, N)
    a_cs = jnp.cumsum(a, axis=-1)
    seg = a_cs[..., :, None] - a_cs[..., None, :]
    causal = jnp.tril(jnp.ones((Q, Q), dtype=bool))
    Lmat = jnp.exp(jnp.where(causal, seg, -jnp.inf))
    CB = jnp.einsum('bclgn,bcsgn->bcgls', Cc, Bc)
    Wdiag = CB[:, :, :, None] * Lmat
    y_diag = jnp.einsum('bcgrls,bcsgrp->bclgrp', Wdiag, xdt)
    decay_states = jnp.exp(a_cs[..., -1:] - a_cs).transpose(0, 1, 4, 2, 3)
    states = jnp.einsum('bclgn,bclgrp->bcgrpn', Bc, xdt * decay_states[..., None])
    chunk_decay = jnp.exp(a_cs[..., -1])

    def step(hstate, inp):
        st, dec = inp
        return hstate * dec[..., None, None] + st, hstate

    h0 = jnp.zeros((b, G, R, P, N), dtype=states.dtype)
    _, prev = lax.scan(step, h0, (states.transpose(1, 0, 2, 3, 4, 5), chunk_decay.transpose(1, 0, 2, 3)))
    prev = prev.transpose(1, 0, 2, 3, 4, 5)
    decay_out = jnp.exp(a_cs).transpose(0, 1, 4, 2, 3)
    y_off = jnp.einsum('bclgn,bcgrpn->bclgrp', Cc, prev) * decay_out[..., None]
    return (y_diag + y_off).reshape(b, s, nh, P)


def ssd_branch(z, xBC, dt_raw, conv_w, conv_b, dt_bias, a_log, d_skip, norm_g):
    b, s, _ = z.shape
    xBC = jax.nn.silu(causal_depthwise_conv(xBC, conv_w, conv_b))
    xs = xBC[..., :SSD_INNER].reshape(b, s, SSD_HEADS, SSD_HEAD_DIM)
    Bm = xBC[..., SSD_INNER:SSD_INNER + SSD_GROUPS * SSD_STATE].reshape(b, s, SSD_GROUPS, SSD_STATE)
    Cm = xBC[..., SSD_INNER + SSD_GROUPS * SSD_STATE:].reshape(b, s, SSD_GROUPS, SSD_STATE)
    dt = jax.nn.softplus(dt_raw.astype(jnp.float32) + dt_bias.astype(jnp.float32))
    A = -jnp.exp(a_log.astype(jnp.float32))
    y = ssd_chunked(xs, dt, A, Bm, Cm) + d_skip.astype(jnp.float32)[:, None] * xs
    y = (y.reshape(b, s, SSD_INNER).astype(z.dtype)) * jax.nn.silu(z)
    yg = rmsnorm(y.reshape(b, s, SSD_GROUPS, SSD_INNER // SSD_GROUPS),
                 jnp.ones((SSD_INNER // SSD_GROUPS,), dtype=y.dtype))
    return yg.reshape(b, s, SSD_INNER) * norm_g


def memory_cross_attention(u, mem_n, wq, wk, wv, wo):
    b, s, _ = u.shape
    m = mem_n.shape[1]
    q = (u @ wq).reshape(b, s, XA_HEADS, XA_HEAD_DIM)
    k = (mem_n @ wk).reshape(b, m, XA_HEADS, XA_HEAD_DIM)
    v = (mem_n @ wv).reshape(b, m, XA_HEADS, XA_HEAD_DIM)
    sc = jnp.einsum('bshd,bmhd->bhsm', q, k).astype(jnp.float32) * (XA_HEAD_DIM ** -0.5)
    p = jax.nn.softmax(sc, axis=-1).astype(v.dtype)
    o = jnp.einsum('bhsm,bmhd->bshd', p, v).reshape(b, s, XA_HEADS * XA_HEAD_DIM)
    return o @ wo


def setup_inputs(seed: int = 0) -> dict:
    key = jax.random.key(seed)
    ks = jax.random.split(key, 32)
    f32 = jnp.float32

    def nrm(k, shape, fan_in):
        return jax.random.normal(k, shape, f32) * (fan_in ** -0.5)

    dt0 = jnp.exp(jax.random.uniform(ks[10], (DEPTH, SSD_HEADS), f32, math.log(1e-3), math.log(1e-1)))
    return {
        "x": jax.random.normal(ks[0], (BATCH, SEQ, D_MODEL), f32),
        "mem": jax.random.normal(ks[1], (BATCH, N_MEM, D_MODEL), f32),
        "norm_g": 1.0 + 0.02 * jax.random.normal(ks[2], (DEPTH, N_SUBNORMS, D_MODEL), f32),
        "ffn1_gate": nrm(ks[3], (DEPTH, D_MODEL, D_FF), D_MODEL),
        "ffn1_up": nrm(ks[4], (DEPTH, D_MODEL, D_FF), D_MODEL),
        "ffn1_down": nrm(ks[5], (DEPTH, D_FF, D_MODEL), D_FF),
        "w_in": nrm(ks[6], (DEPTH, D_MODEL, IN_WIDTH), D_MODEL),
        "b_gate": 0.01 * jax.random.normal(ks[7], (DEPTH, N_BRANCH * D_MODEL), f32),
        "conv_w": nrm(ks[8], (DEPTH, SSD_CONV, SSD_CONV_CH), SSD_CONV),
        "conv_b": 0.01 * jax.random.normal(ks[9], (DEPTH, SSD_CONV_CH), f32),
        "dt_bias": dt0 + jnp.log(-jnp.expm1(-dt0)),
        "a_log": jnp.log(jax.random.uniform(ks[11], (DEPTH, SSD_HEADS), f32, 1.0, 16.0)),
        "d_skip": 1.0 + 0.02 * jax.random.normal(ks[12], (DEPTH, SSD_HEADS), f32),
        "ssd_norm_g": 1.0 + 0.02 * jax.random.normal(ks[13], (DEPTH, SSD_INNER), f32),
        "w_att_out": nrm(ks[14], (DEPTH, ATT_OUT, D_MODEL), ATT_OUT),
        "w_ssd_out": nrm(ks[15], (DEPTH, SSD_INNER, D_MODEL), SSD_INNER),
        "w_o": nrm(ks[16], (DEPTH, D_MODEL, D_MODEL), D_MODEL),
        "mem_norm_g": 1.0 + 0.02 * jax.random.normal(ks[17], (DEPTH, D_MODEL), f32),
        "xa_wq": nrm(ks[18], (DEPTH, D_MODEL, XA_HEADS * XA_HEAD_DIM), D_MODEL),
        "xa_wk": nrm(ks[19], (DEPTH, D_MODEL, XA_HEADS * XA_HEAD_DIM), D_MODEL),
        "xa_wv": nrm(ks[20], (DEPTH, D_MODEL, XA_HEADS * XA_HEAD_DIM), D_MODEL),
        "xa_wo": nrm(ks[21], (DEPTH, XA_HEADS * XA_HEAD_DIM, D_MODEL), XA_HEADS * XA_HEAD_DIM),
        "ffn2_gate": nrm(ks[22], (DEPTH, D_MODEL, D_FF), D_MODEL),
        "ffn2_up": nrm(ks[23], (DEPTH, D_MODEL, D_FF), D_MODEL),
        "ffn2_down": nrm(ks[24], (DEPTH, D_FF, D_MODEL), D_FF),
    }


def reference(x, mem, norm_g, ffn1_gate, ffn1_up, ffn1_down, w_in, b_gate, conv_w, conv_b,
              dt_bias, a_log, d_skip, ssd_norm_g, w_att_out, w_ssd_out, w_o, mem_norm_g,
              xa_wq, xa_wk, xa_wv, xa_wo, ffn2_gate, ffn2_up, ffn2_down):
    b, s, _ = x.shape
    sizes = (ATT_WIDTH, ATT_WIDTH, ATT_WIDTH, SSD_INNER, SSD_CONV_CH, SSD_HEADS, N_BRANCH * D_MODEL)
    cuts = []
    acc = 0
    for sz in sizes[:-1]:
        acc += sz
        cuts.append(acc)
    h = x
    for l in range(DEPTH):
        g = norm_g[l]
        h = h + 0.5 * rmsnorm(swiglu(rmsnorm(h, g[0]), ffn1_gate[l], ffn1_up[l], ffn1_down[l]), g[1])
        u = rmsnorm(h, g[2])
        q, k, v, z, xBC, dt_raw, gate_pre = jnp.split(u @ w_in[l], cuts, axis=-1)
        hshape = (b, s, N_DIL, HEADS_PER_GROUP, HEAD_DIM)
        y_att = dilated_attention(q.reshape(hshape), k.reshape(hshape), v.reshape(hshape)).reshape(b, s, ATT_OUT)
        y_ssd = ssd_branch(z, xBC, dt_raw, conv_w[l], conv_b[l], dt_bias[l], a_log[l], d_skip[l], ssd_norm_g[l])
        gates = jax.nn.sigmoid(gate_pre + b_gate[l]).reshape(b, s, N_BRANCH, D_MODEL)
        merged = gates[:, :, 0] * (y_att @ w_att_out[l]) + gates[:, :, 1] * (y_ssd @ w_ssd_out[l])
        h = h + rmsnorm(merged @ w_o[l], g[3])
        mem_n = rmsnorm(mem, mem_norm_g[l])
        h = h + rmsnorm(memory_cross_attention(rmsnorm(h, g[4]), mem_n, xa_wq[l], xa_wk[l], xa_wv[l], xa_wo[l]), g[5])
        h = h + 0.5 * rmsnorm(swiglu(rmsnorm(h, g[6]), ffn2_gate[l], ffn2_up[l], ffn2_down[l]), g[7])
    return h
```

```python
import functools

import jax
import jax.numpy as jnp
from jax import lax
from jax.experimental import pallas as pl
from jax.experimental.pallas import tpu as pltpu

F32 = jnp.float32
BF16 = jnp.bfloat16

EPS = 1e-6
LANES = 128
VMEM_LIMIT = 56 * 1024 * 1024

HEAD_DIM = 64
DIL_GROUPS = ((128, 1), (512, 4), (2048, 16))
N_DIL = len(DIL_GROUPS)
HEADS_PER_GROUP = 8
ATT_WIDTH = N_DIL * HEADS_PER_GROUP * HEAD_DIM
ATT_OUT = HEADS_PER_GROUP * HEAD_DIM
ATT_BLK = 128
XA_HEADS = 4
SSD_HEAD_DIM = 64
SSD_STATE = 128
SSD_GROUPS = 8
SSD_CONV = 4
SSD_CHUNK = 128
N_BRANCH = 2

D_MODEL = 1024
SSD_INNER = 2 * D_MODEL
SSD_HEADS = SSD_INNER // SSD_HEAD_DIM
SSD_CONV_CH = SSD_INNER + 2 * SSD_GROUPS * SSD_STATE
COL_XBC = 0
COL_Z = COL_XBC + SSD_CONV_CH
COL_GATE = COL_Z + SSD_INNER
COL_Q = COL_GATE + N_BRANCH * D_MODEL
COL_K = COL_Q + ATT_WIDTH
COL_V = COL_K + ATT_WIDTH
COL_DT = COL_V + ATT_WIDTH
PROJ_TN = 1024
PROJ_WIDTH = -(-(COL_DT + SSD_HEADS) // PROJ_TN) * PROJ_TN


def _cparams(*sem):
    return pltpu.CompilerParams(dimension_semantics=sem, vmem_limit_bytes=VMEM_LIMIT)


def _resident(shape):
    nd = len(shape)
    return pl.BlockSpec(shape, lambda *_: (0,) * nd, pipeline_mode=pl.Buffered(1))


def _rms(x, g):
    return x * lax.rsqrt(jnp.mean(x * x, axis=-1, keepdims=True) + EPS) * g


def _silu(x):
    return x * (1.0 / (1.0 + jnp.exp(-x)))


def _dot(a, b):
    return jnp.dot(a, b, preferred_element_type=F32)


def _dot_nt(a, b):
    return lax.dot_general(a, b, (((1,), (1,)), ((), ())), preferred_element_type=F32)


def _ffn_kernel(x_ref, gin_ref, gout_ref, wg_ref, wu_ref, wd_ref, o_ref, acc_ref):
    x = x_ref[...]
    xn = _rms(x, gin_ref[...]).astype(BF16)
    acc_ref[...] = jnp.zeros_like(acc_ref)

    def chunk(c, carry):
        g = _dot(xn, wg_ref[c])
        u = _dot(xn, wu_ref[c])
        a = (_silu(g) * u).astype(BF16)
        acc_ref[...] += _dot(a, wd_ref[c])
        return carry

    lax.fori_loop(0, wg_ref.shape[0], chunk, 0)
    o_ref[...] = x + 0.5 * _rms(acc_ref[...], gout_ref[...])


def _ffn(h, g_in, g_out, wg, wu, wd, *, tm=512):
    t, d = h.shape
    nc, _, fc = wg.shape
    return pl.pallas_call(
        _ffn_kernel,
        grid=(t // tm,),
        in_specs=[
            pl.BlockSpec((tm, d), lambda i: (i, 0)),
            _resident((1, d)), _resident((1, d)),
            _resident((nc, d, fc)), _resident((nc, d, fc)), _resident((nc, fc, d)),
        ],
        out_specs=pl.BlockSpec((tm, d), lambda i: (i, 0)),
        out_shape=jax.ShapeDtypeStruct((t, d), F32),
        scratch_shapes=[pltpu.VMEM((tm, d), F32)],
        compiler_params=_cparams("parallel"),
    )(h, g_in, g_out, wg, wu, wd)


def _proj_kernel(x_ref, g_ref, w_ref, o_ref, xn_ref):
    @pl.when(pl.program_id(1) == 0)
    def _():
        xn_ref[...] = _rms(x_ref[...], g_ref[...]).astype(BF16)

    o_ref[...] = _dot(xn_ref[...], w_ref[...])


def _proj(h, g, w, *, tm=1024, tn=PROJ_TN):
    t, d = h.shape
    n = w.shape[1]
    return pl.pallas_call(
        _proj_kernel,
        grid=(t // tm, n // tn),
        in_specs=[
            pl.BlockSpec((tm, d), lambda i, j: (i, 0)),
            pl.BlockSpec((1, d), lambda i, j: (0, 0)),
            pl.BlockSpec((d, tn), lambda i, j: (0, j)),
        ],
        out_specs=pl.BlockSpec((tm, tn), lambda i, j: (i, j)),
        out_shape=jax.ShapeDtypeStruct((t, n), F32),
        scratch_shapes=[pltpu.VMEM((tm, d), BF16)],
        compiler_params=_cparams("parallel", "arbitrary"),
    )(h, g, w)


def _att_group(q_ref, k_ref, v_ref, o_scr, l_scr, dil):
    seq = q_ref.shape[0]
    nb = seq // (dil * ATT_BLK)
    span = ATT_BLK * dil
    row = lax.broadcasted_iota(jnp.int32, (ATT_BLK, ATT_BLK), 0)
    col = lax.broadcasted_iota(jnp.int32, (ATT_BLK, ATT_BLK), 1)
    keep_prev = col >= row
    keep_cur = col <= row
    lane = lax.broadcasted_iota(jnp.int32, (ATT_BLK, LANES), 1)
    first_head = lane < HEAD_DIM
    scale = HEAD_DIM ** -0.5

    def rows(start):
        return pl.ds(start, ATT_BLK, stride=dil) if dil > 1 else pl.ds(start, ATT_BLK)

    def block(it, carry):
        r = it // nb
        n = it % nb
        cur = r + n * span
        prev = r + jnp.maximum(n - 1, 0) * span
        has_prev = n > 0
        q = q_ref[rows(cur), :]
        k_all = jnp.concatenate([k_ref[rows(prev), :], k_ref[rows(cur), :]], axis=0).astype(BF16)
        v_all = jnp.concatenate([v_ref[rows(prev), :], v_ref[rows(cur), :]], axis=0).astype(BF16)
        keep = jnp.concatenate([jnp.logical_and(keep_prev, has_prev), keep_cur], axis=1)
        outs, lses = [], []
        for head_sel in (first_head, jnp.logical_not(first_head)):
            qh = jnp.where(head_sel, q, 0.0).astype(BF16)
            s = jnp.where(keep, _dot_nt(qh, k_all) * scale, -jnp.inf)
            m = jnp.max(s, axis=-1, keepdims=True)
            p = jnp.exp(s - m)
            den = jnp.sum(p, axis=-1, keepdims=True)
            outs.append(_dot((p * (1.0 / den)).astype(BF16), v_all))
            lses.append(m + jnp.log(den))
        o_scr[rows(cur), :] = jnp.where(first_head, outs[0], outs[1])
        l_scr[rows(cur), :] = jnp.where(first_head, lses[0], lses[1])
        return carry

    lax.fori_loop(0, dil * nb, block, 0)


def _att_kernel(q_ref, k_ref, v_ref, y_ref, o_scr, l_scr):
    g = pl.program_id(2)
    for gi, (_, dil) in enumerate(DIL_GROUPS):
        @pl.when(g == gi)
        def _(gi=gi, dil=dil):
            _att_group(q_ref, k_ref, v_ref, o_scr.at[gi], l_scr.at[gi], dil)

    @pl.when(g == N_DIL - 1)
    def _():
        ls = [l_scr[gi] for gi in range(N_DIL)]
        m = functools.reduce(jnp.maximum, ls)
        es = [jnp.exp(l - m) for l in ls]
        inv = 1.0 / functools.reduce(lambda a, b: a + b, es)
        y = functools.reduce(lambda a, b: a + b, [(e * inv) * o_scr[gi] for gi, e in enumerate(es)])
        y_ref[...] = y.astype(y_ref.dtype)


def _dilated_attention(proj3):
    b, s, _ = proj3.shape
    pairs = ATT_OUT // LANES
    per_group = ATT_OUT // LANES

    def spec(col0):
        return pl.BlockSpec((None, s, LANES), lambda bi, p, g: (bi, 0, col0 // LANES + g * per_group + p))

    return pl.pallas_call(
        _att_kernel,
        grid=(b, pairs, N_DIL),
        in_specs=[spec(COL_Q), spec(COL_K), spec(COL_V)],
        out_specs=pl.BlockSpec((None, s, LANES), lambda bi, p, g: (bi, 0, p)),
        out_shape=jax.ShapeDtypeStruct((b, s, ATT_OUT), BF16),
        scratch_shapes=[pltpu.VMEM((N_DIL, s, LANES), F32), pltpu.VMEM((N_DIL, s, LANES), F32)],
        compiler_params=_cparams("parallel", "parallel", "arbitrary"),
    )(proj3, proj3, proj3)


def _expand_heads(cols, h0):
    q = cols.shape[0]
    lane = lax.broadcasted_iota(jnp.int32, (q, LANES), 1)
    first = lane < SSD_HEAD_DIM
    halves = []
    for pair in range(2):
        a = jnp.broadcast_to(cols[:, h0 + 2 * pair:h0 + 2 * pair + 1], (q, LANES))
        b = jnp.broadcast_to(cols[:, h0 + 2 * pair + 1:h0 + 2 * pair + 2], (q, LANES))
        halves.append(jnp.where(first, a, b))
    return jnp.concatenate(halves, axis=1)


def _ssd_kernel(xbc_ref, z_ref, dt_ref, cw_ref, cb_ref, dtb_ref, alog_ref, dskip_ref, ng_ref, y_ref,
                xs_ref, xc_ref, st_ref):
    q = SSD_CHUNK
    c = pl.program_id(1)
    tail = xs_ref.shape[0] - q

    @pl.when(c == 0)
    def _():
        xs_ref[0:tail, :] = jnp.zeros((tail, xs_ref.shape[1]), F32)
        st_ref[...] = jnp.zeros_like(st_ref)

    xs_ref[tail:tail + q, :] = xbc_ref[...]
    acc = cb_ref[...] + cw_ref[SSD_CONV - 1:SSD_CONV, :] * xs_ref[tail:tail + q, :]
    for kk in range(SSD_CONV - 1):
        off = tail - (SSD_CONV - 1) + kk
        acc = acc + cw_ref[kk:kk + 1, :] * xs_ref[off:off + q, :]
    xc_ref[...] = _silu(acc)
    xs_ref[0:tail, :] = xs_ref[q:q + tail, :]

    dtv = dt_ref[...] + dtb_ref[...]
    dtv = jnp.maximum(dtv, 0.0) + jnp.log1p(jnp.exp(-jnp.abs(dtv)))
    a_cs = dtv * (-jnp.exp(alog_ref[...]))
    rowi = lax.broadcasted_iota(jnp.int32, (q, LANES), 0)
    shift = 1
    while shift < q:
        a_cs = a_cs + jnp.where(rowi >= shift, pltpu.roll(a_cs, shift, axis=0), 0.0)
        shift *= 2
    a_cs_t = a_cs.T

    li = lax.broadcasted_iota(jnp.int32, (q, q), 0)
    si = lax.broadcasted_iota(jnp.int32, (q, q), 1)
    causal = li >= si
    lane = lax.broadcasted_iota(jnp.int32, (q, LANES), 1)
    first = lane < SSD_HEAD_DIM
    heads_per_group = SSD_HEADS // SSD_GROUPS
    gw = heads_per_group * SSD_HEAD_DIM
    b0 = SSD_INNER
    c0 = SSD_INNER + SSD_GROUPS * SSD_STATE

    for g in range(SSD_GROUPS):
        h0 = g * heads_per_group
        xg = xc_ref[:, g * gw:(g + 1) * gw]
        bg = xc_ref[:, b0 + g * SSD_STATE:b0 + (g + 1) * SSD_STATE]
        cg = xc_ref[:, c0 + g * SSD_STATE:c0 + (g + 1) * SSD_STATE].astype(BF16)
        dt_e = _expand_heads(dtv, h0)
        acs_e = _expand_heads(a_cs, h0)
        last_e = acs_e[q - 1:q, :]
        xdt = xg * dt_e
        cb = _dot_nt(cg, bg.astype(BF16))

        y_halves = []
        for pair in range(2):
            xdt_pair = xdt[:, pair * LANES:(pair + 1) * LANES].astype(BF16)
            ys = []
            for j in range(2):
                hh = h0 + 2 * pair + j
                seg = a_cs[:, hh:hh + 1] - a_cs_t[hh:hh + 1, :]
                w = (cb * jnp.exp(jnp.where(causal, seg, -jnp.inf))).astype(BF16)
                ys.append(_dot(w, xdt_pair))
            y_halves.append(jnp.where(first, ys[0], ys[1]))
        y = jnp.concatenate(y_halves, axis=1)

        prev = st_ref[g]
        y = y + _dot(cg, prev.astype(BF16)) * jnp.exp(acs_e)
        upd = _dot(bg.T.astype(BF16), (xdt * jnp.exp(last_e - acs_e)).astype(BF16))
        st_ref[g] = prev * jnp.exp(last_e) + upd

        y = y + dskip_ref[:, g * gw:(g + 1) * gw] * xg
        y = y * _silu(z_ref[:, g * gw:(g + 1) * gw])
        y = y * lax.rsqrt(jnp.mean(y * y, axis=-1, keepdims=True) + EPS)
        y_ref[:, g * gw:(g + 1) * gw] = (y * ng_ref[:, g * gw:(g + 1) * gw]).astype(y_ref.dtype)


def _ssd(proj3, conv_w, conv_b, dt_bias, a_log, d_skip, norm_g):
    b, s, _ = proj3.shape
    q = SSD_CHUNK
    tail = 8
    return pl.pallas_call(
        _ssd_kernel,
        grid=(b, s // q),
        in_specs=[
            pl.BlockSpec((None, q, SSD_CONV_CH), lambda bi, c: (bi, c, COL_XBC // SSD_CONV_CH)),
            pl.BlockSpec((None, q, SSD_INNER), lambda bi, c: (bi, c, COL_Z // SSD_INNER)),
            pl.BlockSpec((None, q, LANES), lambda bi, c: (bi, c, COL_DT // LANES)),
            _resident((SSD_CONV, SSD_CONV_CH)), _resident((1, SSD_CONV_CH)),
            _resident((1, LANES)), _resident((1, LANES)),
            _resident((1, SSD_INNER)), _resident((1, SSD_INNER)),
        ],
        out_specs=pl.BlockSpec((None, q, SSD_INNER), lambda bi, c: (bi, c, 0)),
        out_shape=jax.ShapeDtypeStruct((b, s, SSD_INNER), BF16),
        scratch_shapes=[
            pltpu.VMEM((tail + q, SSD_CONV_CH), F32),
            pltpu.VMEM((q, SSD_CONV_CH), F32),
            pltpu.VMEM((SSD_GROUPS, SSD_STATE, SSD_INNER // SSD_GROUPS), F32),
        ],
        compiler_params=_cparams("parallel", "arbitrary"),
    )(proj3, proj3, proj3, conv_w, conv_b, dt_bias, a_log, d_skip, norm_g)


def _merge_kernel(h_ref, ya_ref, ys_ref, g0_ref, g1_ref, bg_ref, wa_ref, ws_ref, wo_ref, gn_ref, o_ref):
    d = h_ref.shape[1]
    gate0 = 1.0 / (1.0 + jnp.exp(-(g0_ref[...] + bg_ref[:, 0:d])))
    gate1 = 1.0 / (1.0 + jnp.exp(-(g1_ref[...] + bg_ref[:, d:2 * d])))
    merged = gate0 * _dot(ya_ref[...], wa_ref[...]) + gate1 * _dot(ys_ref[...], ws_ref[...])
    o_ref[...] = h_ref[...] + _rms(_dot(merged.astype(BF16), wo_ref[...]), gn_ref[...])


def _merge(h, y_att, y_ssd, proj, b_gate, wa, ws, wo, gn, *, tm=512):
    t, d = h.shape
    return pl.pallas_call(
        _merge_kernel,
        grid=(t // tm,),
        in_specs=[
            pl.BlockSpec((tm, d), lambda i: (i, 0)),
            pl.BlockSpec((tm, ATT_OUT), lambda i: (i, 0)),
            pl.BlockSpec((tm, SSD_INNER), lambda i: (i, 0)),
            pl.BlockSpec((tm, d), lambda i: (i, COL_GATE // d)),
            pl.BlockSpec((tm, d), lambda i: (i, COL_GATE // d + 1)),
            _resident((1, N_BRANCH * d)),
            _resident(wa.shape), _resident(ws.shape), _resident(wo.shape),
            _resident((1, d)),
        ],
        out_specs=pl.BlockSpec((tm, d), lambda i: (i, 0)),
        out_shape=jax.ShapeDtypeStruct((t, d), F32),
        compiler_params=_cparams("parallel"),
    )(h, y_att, y_ssd, proj, proj, b_gate, wa, ws, wo, gn)


def _memkv_kernel(mem_ref, g_ref, wk_ref, wv_ref, k_ref, v_ref):
    mn = _rms(mem_ref[...], g_ref[...]).astype(BF16)
    k_ref[...] = _dot(mn, wk_ref[...]).astype(k_ref.dtype)
    v_ref[...] = _dot(mn, wv_ref[...]).astype(v_ref.dtype)


def _memkv(mem, g, wk, wv):
    b, m, d = mem.shape
    blk = pl.BlockSpec((None, m, d), lambda bi: (bi, 0, 0))
    return pl.pallas_call(
        _memkv_kernel,
        grid=(b,),
        in_specs=[blk, _resident((1, d)), _resident(wk.shape), _resident(wv.shape)],
        out_specs=[blk, blk],
        out_shape=[jax.ShapeDtypeStruct((b, m, d), BF16)] * 2,
        compiler_params=_cparams("parallel"),
    )(mem, g, wk, wv)


def _xattn_kernel(h_ref, k_ref, v_ref, gin_ref, gout_ref, wq_ref, wo_ref, o_ref):
    h = h_ref[...]
    d = h.shape[1]
    hd = d // XA_HEADS
    q = _dot(_rms(h, gin_ref[...]).astype(BF16), wq_ref[...]).astype(BF16)
    outs = []
    for hh in range(XA_HEADS):
        cols = slice(hh * hd, (hh + 1) * hd)
        s = _dot_nt(q[:, cols], k_ref[:, cols]) * (hd ** -0.5)
        p = jnp.exp(s - jnp.max(s, axis=-1, keepdims=True))
        p = p * (1.0 / jnp.sum(p, axis=-1, keepdims=True))
        outs.append(_dot(p.astype(BF16), v_ref[:, cols]))
    o = jnp.concatenate(outs, axis=1).astype(BF16)
    o_ref[...] = h + _rms(_dot(o, wo_ref[...]), gout_ref[...])


def _xattn(h3, k, v, g_in, g_out, wq, wo, *, tm=512):
    b, s, d = h3.shape
    m = k.shape[1]
    tok = pl.BlockSpec((None, tm, d), lambda bi, i: (bi, i, 0))
    kv = pl.BlockSpec((None, m, d), lambda bi, i: (bi, 0, 0))
    return pl.pallas_call(
        _xattn_kernel,
        grid=(b, s // tm),
        in_specs=[tok, kv, kv, _resident((1, d)), _resident((1, d)), _resident(wq.shape), _resident(wo.shape)],
        out_specs=tok,
        out_shape=jax.ShapeDtypeStruct((b, s, d), F32),
        compiler_params=_cparams("parallel", "parallel"),
    )(h3, k, v, g_in, g_out, wq, wo)


def _chunk_cols(w, fc):
    d, f = w.shape
    return w.reshape(d, f // fc, fc).transpose(1, 0, 2)


def _reorder_w_in(w):
    o_q, o_k, o_v = 0, ATT_WIDTH, 2 * ATT_WIDTH
    o_z = 3 * ATT_WIDTH
    o_xbc = o_z + SSD_INNER
    o_dt = o_xbc + SSD_CONV_CH
    o_gate = o_dt + SSD_HEADS
    parts = [w[:, o_xbc:o_dt], w[:, o_z:o_xbc], w[:, o_gate:], w[:, o_q:o_k], w[:, o_k:o_v], w[:, o_v:o_z],
             w[:, o_dt:o_gate]]
    used = sum(p.shape[1] for p in parts)
    parts.append(jnp.zeros((w.shape[0], PROJ_WIDTH - used), w.dtype))
    return jnp.concatenate(parts, axis=1)


def _pad_lanes(v):
    return jnp.pad(v, (0, LANES - v.shape[0])).reshape(1, LANES)


def kernel(x, mem, norm_g, ffn1_gate, ffn1_up, ffn1_down, w_in, b_gate, conv_w, conv_b, dt_bias, a_log, d_skip,
           ssd_norm_g, w_att_out, w_ssd_out, w_o, mem_norm_g, xa_wq, xa_wk, xa_wv, xa_wo, ffn2_gate, ffn2_up,
           ffn2_down):
    b, s, d = x.shape
    depth = norm_g.shape[0]
    d_ff = ffn1_gate.shape[2]
    fc = 256
    assert d == D_MODEL and d_ff % fc == 0 and s % (DIL_GROUPS[-1][1] * ATT_BLK) == 0

    h = x.reshape(b * s, d)
    for l in range(depth):
        g = norm_g[l].reshape(-1, 1, d)
        h = _ffn(h, g[0], g[1], _chunk_cols(ffn1_gate[l].astype(BF16), fc), _chunk_cols(ffn1_up[l].astype(BF16), fc),
                 ffn1_down[l].astype(BF16).reshape(d_ff // fc, fc, d))
        proj = _proj(h, g[2], _reorder_w_in(w_in[l]).astype(BF16))
        proj3 = proj.reshape(b, s, PROJ_WIDTH)
        y_att = _dilated_attention(proj3).reshape(b * s, ATT_OUT)
        y_ssd = _ssd(proj3, conv_w[l], conv_b[l].reshape(1, -1), _pad_lanes(dt_bias[l]), _pad_lanes(a_log[l]),
                     jnp.repeat(d_skip[l], SSD_HEAD_DIM).reshape(1, -1), ssd_norm_g[l].reshape(1, -1))
        h = _merge(h, y_att, y_ssd.reshape(b * s, SSD_INNER), proj, b_gate[l].reshape(1, -1),
                   w_att_out[l].astype(BF16), w_ssd_out[l].astype(BF16), w_o[l].astype(BF16), g[3])
        k_mem, v_mem = _memkv(mem, mem_norm_g[l].reshape(1, d), xa_wk[l].astype(BF16), xa_wv[l].astype(BF16))
        h = _xattn(h.reshape(b, s, d), k_mem, v_mem, g[4], g[5], xa_wq[l].astype(BF16),
                   xa_wo[l].astype(BF16)).reshape(b * s, d)
        h = _ffn(h, g[6], g[7], _chunk_cols(ffn2_gate[l].astype(BF16), fc), _chunk_cols(ffn2_up[l].astype(BF16), fc),
                 ffn2_down[l].astype(BF16).reshape(d_ff // fc, fc, d))
    return h.reshape(b, s, d)
```

```python
import functools

import jax
import jax.numpy as jnp
from jax import lax
from jax.experimental import pallas as pl
from jax.experimental.pallas import tpu as pltpu

F32 = jnp.float32
BF16 = jnp.bfloat16

EPS = 1e-6
LANES = 128
VMEM_LIMIT = 56 * 1024 * 1024

HEAD_DIM = 64
DIL_GROUPS = ((128, 1), (512, 4), (2048, 16))
N_DIL = len(DIL_GROUPS)
HEADS_PER_GROUP = 8
ATT_WIDTH = N_DIL * HEADS_PER_GROUP * HEAD_DIM
ATT_OUT = HEADS_PER_GROUP * HEAD_DIM
ATT_BLK = 128
XA_HEADS = 4
SSD_HEAD_DIM = 64
SSD_STATE = 128
SSD_GROUPS = 8
SSD_CONV = 4
SSD_CHUNK = 128
CONV_ROW_STRIDE = 4
N_BRANCH = 2

D_MODEL = 1024
SSD_INNER = 2 * D_MODEL
SSD_HEADS = SSD_INNER // SSD_HEAD_DIM
SSD_CONV_CH = SSD_INNER + 2 * SSD_GROUPS * SSD_STATE
COL_XBC = 0
COL_Z = COL_XBC + SSD_CONV_CH
COL_Q = COL_Z + SSD_INNER
COL_K = COL_Q + ATT_WIDTH
COL_V = COL_K + ATT_WIDTH
COL_DT = COL_V + ATT_WIDTH
PROJ_TN = 1024
PROJ_WIDTH = -(-(COL_DT + SSD_HEADS) // PROJ_TN) * PROJ_TN


def _cparams(*sem):
    return pltpu.CompilerParams(dimension_semantics=sem, vmem_limit_bytes=VMEM_LIMIT)


def _resident(shape):
    nd = len(shape)
    return pl.BlockSpec(shape, lambda *_: (0,) * nd, pipeline_mode=pl.Buffered(1))


def _rms(x, g):
    return x * lax.rsqrt(jnp.mean(x * x, axis=-1, keepdims=True) + EPS) * g


def _silu(x):
    return x * (1.0 / (1.0 + jnp.exp(-x)))


def _dot(a, b):
    return jnp.dot(a, b, preferred_element_type=F32)


def _dot_nt(a, b):
    return lax.dot_general(a, b, (((1,), (1,)), ((), ())), preferred_element_type=F32)


def _ffn_kernel(x_ref, gin_ref, gout_ref, wg_ref, wu_ref, wd_ref, o_ref, *, fc):
    x = x_ref[...]
    xn = _rms(x, gin_ref[...]).astype(BF16)
    acc = None
    for c in range(wg_ref.shape[1] // fc):
        cols = slice(c * fc, (c + 1) * fc)
        a = (_silu(_dot(xn, wg_ref[:, cols])) * _dot(xn, wu_ref[:, cols])).astype(BF16)
        part = _dot(a, wd_ref[cols, :])
        acc = part if acc is None else acc + part
    o_ref[...] = x + 0.5 * _rms(acc, gout_ref[...])


def _ffn(h, g_in, g_out, wg, wu, wd, *, tm=512, fc=256):
    t, d = h.shape
    d_ff = wg.shape[1]
    assert d_ff % fc == 0
    return pl.pallas_call(
        functools.partial(_ffn_kernel, fc=fc),
        grid=(t // tm,),
        in_specs=[
            pl.BlockSpec((tm, d), lambda i: (i, 0)),
            _resident((1, d)), _resident((1, d)),
            _resident((d, d_ff)), _resident((d, d_ff)), _resident((d_ff, d)),
        ],
        out_specs=pl.BlockSpec((tm, d), lambda i: (i, 0)),
        out_shape=jax.ShapeDtypeStruct((t, d), F32),
        compiler_params=_cparams("parallel"),
    )(h, g_in, g_out, wg, wu, wd)


def _proj_kernel(x_ref, g_ref, w_ref, o_ref, xn_ref):
    @pl.when(pl.program_id(1) == 0)
    def _():
        xn_ref[...] = _rms(x_ref[...], g_ref[...]).astype(BF16)

    o_ref[...] = _dot(xn_ref[...], w_ref[...])


def _proj(h, g, w, *, tm=2048, tn=PROJ_TN):
    t, d = h.shape
    n = w.shape[1]
    return pl.pallas_call(
        _proj_kernel,
        grid=(t // tm, n // tn),
        in_specs=[
            pl.BlockSpec((tm, d), lambda i, j: (i, 0)),
            pl.BlockSpec((1, d), lambda i, j: (0, 0)),
            pl.BlockSpec((d, tn), lambda i, j: (0, j)),
        ],
        out_specs=pl.BlockSpec((tm, tn), lambda i, j: (i, j)),
        out_shape=jax.ShapeDtypeStruct((t, n), F32),
        scratch_shapes=[pltpu.VMEM((tm, d), BF16)],
        compiler_params=_cparams("parallel", "arbitrary"),
    )(h, g, w)


def _att_group(q_ref, k_ref, v_ref, o_scr, l_scr, dil):
    seq = q_ref.shape[0]
    nb = seq // (dil * ATT_BLK)
    span = ATT_BLK * dil
    row = lax.broadcasted_iota(jnp.int32, (ATT_BLK, ATT_BLK), 0)
    col = lax.broadcasted_iota(jnp.int32, (ATT_BLK, ATT_BLK), 1)
    keep_prev = col >= row
    keep_cur = col <= row
    lane = lax.broadcasted_iota(jnp.int32, (ATT_BLK, LANES), 1)
    first_head = lane < HEAD_DIM
    scale = HEAD_DIM ** -0.5

    keep_both = jnp.concatenate([keep_prev, keep_cur], axis=1)

    def rows(start):
        return pl.ds(start, ATT_BLK, stride=dil) if dil > 1 else pl.ds(start, ATT_BLK)

    def residue(r):
        k_prev = v_prev = None
        for n in range(nb):
            cur = rows(r + n * span)
            q = q_ref[cur, :] * scale
            k_cur = k_ref[cur, :].astype(BF16)
            v_cur = v_ref[cur, :].astype(BF16)
            if n == 0:
                k_all, v_all, keep = k_cur, v_cur, keep_cur
            else:
                k_all = jnp.concatenate([k_prev, k_cur], axis=0)
                v_all = jnp.concatenate([v_prev, v_cur], axis=0)
                keep = keep_both
            k_prev, v_prev = k_cur, v_cur
            ms, dens, outs = [], [], []
            for head_sel in (first_head, jnp.logical_not(first_head)):
                qh = jnp.where(head_sel, q, 0.0).astype(BF16)
                s = jnp.where(keep, _dot_nt(qh, k_all), -jnp.inf)
                m = jnp.max(s, axis=-1, keepdims=True)
                p = jnp.exp(s - m)
                ms.append(m)
                dens.append(jnp.sum(p, axis=-1, keepdims=True))
                outs.append(_dot(p.astype(BF16), v_all))
            den = jnp.where(first_head, dens[0], dens[1])
            o_scr[cur, :] = jnp.where(first_head, outs[0], outs[1]) * (1.0 / den)
            l_scr[cur, :] = jnp.where(first_head, ms[0], ms[1]) + jnp.log(den)

    for r in range(dil):
        residue(r)


def _att_kernel(q_ref, k_ref, v_ref, y_ref, o_scr, l_scr):
    g = pl.program_id(2)
    for gi, (_, dil) in enumerate(DIL_GROUPS):
        @pl.when(g == gi)
        def _(gi=gi, dil=dil):
            _att_group(q_ref, k_ref, v_ref, o_scr.at[gi], l_scr.at[gi], dil)

    @pl.when(g == N_DIL - 1)
    def _():
        ls = [l_scr[gi] for gi in range(N_DIL)]
        m = functools.reduce(jnp.maximum, ls)
        es = [jnp.exp(l - m) for l in ls]
        inv = 1.0 / functools.reduce(lambda a, b: a + b, es)
        y = functools.reduce(lambda a, b: a + b, [(e * inv) * o_scr[gi] for gi, e in enumerate(es)])
        y_ref[...] = y.astype(y_ref.dtype)


def _dilated_attention(proj3):
    b, s, _ = proj3.shape
    pairs = ATT_OUT // LANES
    per_group = ATT_OUT // LANES

    def spec(col0):
        return pl.BlockSpec((None, s, LANES), lambda bi, p, g: (bi, 0, col0 // LANES + g * per_group + p))

    return pl.pallas_call(
        _att_kernel,
        grid=(b, pairs, N_DIL),
        in_specs=[spec(COL_Q), spec(COL_K), spec(COL_V)],
        out_specs=pl.BlockSpec((None, s, LANES), lambda bi, p, g: (bi, 0, p)),
        out_shape=jax.ShapeDtypeStruct((b, s, ATT_OUT), BF16),
        scratch_shapes=[pltpu.VMEM((N_DIL, s, LANES), F32), pltpu.VMEM((N_DIL, s, LANES), F32)],
        compiler_params=_cparams("parallel", "parallel", "arbitrary"),
    )(proj3, proj3, proj3)


def _expand_heads(cols, h0):
    q = cols.shape[0]
    lane = lax.broadcasted_iota(jnp.int32, (q, LANES), 1)
    first = lane < SSD_HEAD_DIM
    halves = []
    for pair in range(2):
        a = jnp.broadcast_to(cols[:, h0 + 2 * pair:h0 + 2 * pair + 1], (q, LANES))
        b = jnp.broadcast_to(cols[:, h0 + 2 * pair + 1:h0 + 2 * pair + 2], (q, LANES))
        halves.append(jnp.where(first, a, b))
    return jnp.concatenate(halves, axis=1)


def _ssd_kernel(xbc_ref, z_ref, dt_ref, cw_ref, cb_ref, dtb_ref, alog_ref, dskip_ref, ng_ref, y_ref,
                xs_ref, xc_ref, st_ref):
    q = SSD_CHUNK
    c = pl.program_id(1)
    n_slab = xs_ref.shape[0]
    tail = xs_ref.shape[1] - q

    @pl.when(c == 0)
    def _():
        xs_ref[:, 0:tail, :] = jnp.zeros((n_slab, tail, LANES), F32)
        st_ref[...] = jnp.zeros_like(st_ref)

    for j in range(n_slab):
        lanes = slice(j * LANES, (j + 1) * LANES)
        xs_ref[j, tail:tail + q, :] = xbc_ref[:, lanes]
        for e in range(CONV_ROW_STRIDE):
            acc = cb_ref[:, lanes]
            for kk in range(SSD_CONV):
                start = tail + e - (SSD_CONV - 1) + kk
                acc = acc + cw_ref[kk:kk + 1, lanes] * xs_ref[j, pl.ds(start, q // CONV_ROW_STRIDE,
                                                                       stride=CONV_ROW_STRIDE), :]
            xc_ref[j, pl.ds(e, q // CONV_ROW_STRIDE, stride=CONV_ROW_STRIDE), :] = _silu(acc)
        xs_ref[j, 0:tail, :] = xs_ref[j, q:q + tail, :]

    dtv = dt_ref[...] + dtb_ref[...]
    dtv = jnp.maximum(dtv, 0.0) + jnp.log1p(jnp.exp(-jnp.abs(dtv)))
    a_cs = dtv * (-jnp.exp(alog_ref[...]))
    rowi = lax.broadcasted_iota(jnp.int32, (q, LANES), 0)
    shift = 1
    while shift < q:
        a_cs = a_cs + jnp.where(rowi >= shift, pltpu.roll(a_cs, shift, axis=0), 0.0)
        shift *= 2
    a_cs_t = a_cs.T

    li = lax.broadcasted_iota(jnp.int32, (q, q), 0)
    si = lax.broadcasted_iota(jnp.int32, (q, q), 1)
    causal = li >= si
    lane = lax.broadcasted_iota(jnp.int32, (q, LANES), 1)
    first = lane < SSD_HEAD_DIM
    heads_per_group = SSD_HEADS // SSD_GROUPS
    gw = heads_per_group * SSD_HEAD_DIM
    b0 = SSD_INNER
    c0 = SSD_INNER + SSD_GROUPS * SSD_STATE

    for g in range(SSD_GROUPS):
        h0 = g * heads_per_group
        xg = jnp.concatenate([xc_ref[g * gw // LANES + i] for i in range(gw // LANES)], axis=1)
        bg = xc_ref[b0 // LANES + g]
        cg = xc_ref[c0 // LANES + g].astype(BF16)
        dt_e = _expand_heads(dtv, h0)
        acs_e = _expand_heads(a_cs, h0)
        last_e = acs_e[q - 1:q, :]
        xdt = xg * dt_e
        cb = _dot_nt(cg, bg.astype(BF16))

        y_halves = []
        for pair in range(2):
            xdt_pair = xdt[:, pair * LANES:(pair + 1) * LANES].astype(BF16)
            ys = []
            for j in range(2):
                hh = h0 + 2 * pair + j
                seg = a_cs[:, hh:hh + 1] - a_cs_t[hh:hh + 1, :]
                w = (cb * jnp.exp(jnp.where(causal, seg, -jnp.inf))).astype(BF16)
                ys.append(_dot(w, xdt_pair))
            y_halves.append(jnp.where(first, ys[0], ys[1]))
        y = jnp.concatenate(y_halves, axis=1)

        prev = st_ref[g]
        y = y + _dot(cg, prev.astype(BF16)) * jnp.exp(acs_e)
        upd = _dot(bg.T.astype(BF16), (xdt * jnp.exp(last_e - acs_e)).astype(BF16))
        st_ref[g] = prev * jnp.exp(last_e) + upd

        y = y + dskip_ref[:, g * gw:(g + 1) * gw] * xg
        y = y * _silu(z_ref[:, g * gw:(g + 1) * gw])
        y = y * lax.rsqrt(jnp.mean(y * y, axis=-1, keepdims=True) + EPS)
        y_ref[:, g * gw:(g + 1) * gw] = (y * ng_ref[:, g * gw:(g + 1) * gw]).astype(y_ref.dtype)


def _ssd(proj3, conv_w, conv_b, dt_bias, a_log, d_skip, norm_g):
    b, s, _ = proj3.shape
    q = SSD_CHUNK
    tail = 8
    return pl.pallas_call(
        _ssd_kernel,
        grid=(b, s // q),
        in_specs=[
            pl.BlockSpec((None, q, SSD_CONV_CH), lambda bi, c: (bi, c, COL_XBC // SSD_CONV_CH)),
            pl.BlockSpec((None, q, SSD_INNER), lambda bi, c: (bi, c, COL_Z // SSD_INNER)),
            pl.BlockSpec((None, q, LANES), lambda bi, c: (bi, c, COL_DT // LANES)),
            _resident((SSD_CONV, SSD_CONV_CH)), _resident((1, SSD_CONV_CH)),
            _resident((1, LANES)), _resident((1, LANES)),
            _resident((1, SSD_INNER)), _resident((1, SSD_INNER)),
        ],
        out_specs=pl.BlockSpec((None, q, SSD_INNER), lambda bi, c: (bi, c, 0)),
        out_shape=jax.ShapeDtypeStruct((b, s, SSD_INNER), BF16),
        scratch_shapes=[
            pltpu.VMEM((SSD_CONV_CH // LANES, tail + q, LANES), F32),
            pltpu.VMEM((SSD_CONV_CH // LANES, q, LANES), F32),
            pltpu.VMEM((SSD_GROUPS, SSD_STATE, SSD_INNER // SSD_GROUPS), F32),
        ],
        compiler_params=_cparams("parallel", "arbitrary"),
    )(proj3, proj3, proj3, conv_w, conv_b, dt_bias, a_log, d_skip, norm_g)


def _merge_kernel(h_ref, ya_ref, ys_ref, gu_ref, wgate_ref, bg_ref, wa_ref, ws_ref, wo_ref, gn_ref, o_ref):
    h = h_ref[...]
    d = h.shape[1]
    gates = _dot(_rms(h, gu_ref[...]).astype(BF16), wgate_ref[...]) + bg_ref[...]
    gates = 1.0 / (1.0 + jnp.exp(-gates))
    merged = gates[:, 0:d] * _dot(ya_ref[...], wa_ref[...]) + gates[:, d:2 * d] * _dot(ys_ref[...], ws_ref[...])
    o_ref[...] = h + _rms(_dot(merged.astype(BF16), wo_ref[...]), gn_ref[...])


def _merge(h, y_att, y_ssd, g_u, w_gate, b_gate, wa, ws, wo, gn, *, tm=512):
    t, d = h.shape
    return pl.pallas_call(
        _merge_kernel,
        grid=(t // tm,),
        in_specs=[
            pl.BlockSpec((tm, d), lambda i: (i, 0)),
            pl.BlockSpec((tm, ATT_OUT), lambda i: (i, 0)),
            pl.BlockSpec((tm, SSD_INNER), lambda i: (i, 0)),
            _resident((1, d)), _resident(w_gate.shape), _resident((1, N_BRANCH * d)),
            _resident(wa.shape), _resident(ws.shape), _resident(wo.shape),
            _resident((1, d)),
        ],
        out_specs=pl.BlockSpec((tm, d), lambda i: (i, 0)),
        out_shape=jax.ShapeDtypeStruct((t, d), F32),
        compiler_params=_cparams("parallel"),
    )(h, y_att, y_ssd, g_u, w_gate, b_gate, wa, ws, wo, gn)


def _memkv_kernel(mem_ref, g_ref, wk_ref, wv_ref, k_ref, v_ref):
    mn = _rms(mem_ref[...], g_ref[...]).astype(BF16)
    k_ref[...] = _dot(mn, wk_ref[...]).astype(k_ref.dtype)
    v_ref[...] = _dot(mn, wv_ref[...]).astype(v_ref.dtype)


def _memkv(mem, g, wk, wv):
    b, m, d = mem.shape
    blk = pl.BlockSpec((None, m, d), lambda bi: (bi, 0, 0))
    return pl.pallas_call(
        _memkv_kernel,
        grid=(b,),
        in_specs=[blk, _resident((1, d)), _resident(wk.shape), _resident(wv.shape)],
        out_specs=[blk, blk],
        out_shape=[jax.ShapeDtypeStruct((b, m, d), BF16)] * 2,
        compiler_params=_cparams("parallel"),
    )(mem, g, wk, wv)


def _xattn_kernel(h_ref, k_ref, v_ref, gin_ref, gout_ref, wq_ref, wo_ref, o_ref):
    h = h_ref[...]
    d = h.shape[1]
    hd = d // XA_HEADS
    q = _dot(_rms(h, gin_ref[...]).astype(BF16), wq_ref[...]).astype(BF16)
    outs = []
    for hh in range(XA_HEADS):
        cols = slice(hh * hd, (hh + 1) * hd)
        s = _dot_nt(q[:, cols], k_ref[:, cols]) * (hd ** -0.5)
        p = jnp.exp(s - jnp.max(s, axis=-1, keepdims=True))
        p = p * (1.0 / jnp.sum(p, axis=-1, keepdims=True))
        outs.append(_dot(p.astype(BF16), v_ref[:, cols]))
    o = jnp.concatenate(outs, axis=1).astype(BF16)
    o_ref[...] = h + _rms(_dot(o, wo_ref[...]), gout_ref[...])


def _xattn(h3, k, v, g_in, g_out, wq, wo, *, tm=512):
    b, s, d = h3.shape
    m = k.shape[1]
    tok = pl.BlockSpec((None, tm, d), lambda bi, i: (bi, i, 0))
    kv = pl.BlockSpec((None, m, d), lambda bi, i: (bi, 0, 0))
    return pl.pallas_call(
        _xattn_kernel,
        grid=(b, s // tm),
        in_specs=[tok, kv, kv, _resident((1, d)), _resident((1, d)), _resident(wq.shape), _resident(wo.shape)],
        out_specs=tok,
        out_shape=jax.ShapeDtypeStruct((b, s, d), F32),
        compiler_params=_cparams("parallel", "parallel"),
    )(h3, k, v, g_in, g_out, wq, wo)


def _split_w_in(w):
    o_z = 3 * ATT_WIDTH
    o_xbc = o_z + SSD_INNER
    o_dt = o_xbc + SSD_CONV_CH
    o_gate = o_dt + SSD_HEADS
    parts = [w[:, o_xbc:o_dt], w[:, o_z:o_xbc], w[:, :o_z], w[:, o_dt:o_gate]]
    used = sum(p.shape[1] for p in parts)
    parts.append(jnp.zeros((w.shape[0], PROJ_WIDTH - used), w.dtype))
    return jnp.concatenate(parts, axis=1), w[:, o_gate:]


def _pad_lanes(v):
    return jnp.pad(v, (0, LANES - v.shape[0])).reshape(1, LANES)


def kernel(x, mem, norm_g, ffn1_gate, ffn1_up, ffn1_down, w_in, b_gate, conv_w, conv_b, dt_bias, a_log, d_skip,
           ssd_norm_g, w_att_out, w_ssd_out, w_o, mem_norm_g, xa_wq, xa_wk, xa_wv, xa_wo, ffn2_gate, ffn2_up,
           ffn2_down):
    b, s, d = x.shape
    depth = norm_g.shape[0]
    assert d == D_MODEL and s % (DIL_GROUPS[-1][1] * ATT_BLK) == 0
    bf = lambda w: w.astype(BF16)

    h = x.reshape(b * s, d)
    for l in range(depth):
        g = norm_g[l].reshape(-1, 1, d)
        h = _ffn(h, g[0], g[1], bf(ffn1_gate[l]), bf(ffn1_up[l]), bf(ffn1_down[l]))
        w_proj, w_gate = _split_w_in(bf(w_in[l]))
        proj3 = _proj(h, g[2], w_proj).reshape(b, s, PROJ_WIDTH)
        y_att = _dilated_attention(proj3).reshape(b * s, ATT_OUT)
        y_ssd = _ssd(proj3, conv_w[l], conv_b[l].reshape(1, -1), _pad_lanes(dt_bias[l]), _pad_lanes(a_log[l]),
                     jnp.repeat(d_skip[l], SSD_HEAD_DIM).reshape(1, -1), ssd_norm_g[l].reshape(1, -1))
        h = _merge(h, y_att, y_ssd.reshape(b * s, SSD_INNER), g[2], w_gate, b_gate[l].reshape(1, -1),
                   bf(w_att_out[l]), bf(w_ssd_out[l]), bf(w_o[l]), g[3])
        k_mem, v_mem = _memkv(mem, mem_norm_g[l].reshape(1, d), bf(xa_wk[l]), bf(xa_wv[l]))
        h = _xattn(h.reshape(b, s, d), k_mem, v_mem, g[4], g[5], bf(xa_wq[l]), bf(xa_wo[l])).reshape(b * s, d)
        h = _ffn(h, g[6], g[7], bf(ffn2_gate[l]), bf(ffn2_up[l]), bf(ffn2_down[l]))
    return h.reshape(b, s, d)
```

```python
import functools

import jax
import jax.numpy as jnp
from jax import lax
from jax.experimental import pallas as pl
from jax.experimental.pallas import tpu as pltpu

F32 = jnp.float32
BF16 = jnp.bfloat16

EPS = 1e-6
LOG2E = 1.4426950408889634
LANES = 128
VMEM_LIMIT = 56 * 1024 * 1024

HEAD_DIM = 64
DIL_GROUPS = ((128, 1), (512, 4), (2048, 16))
N_DIL = len(DIL_GROUPS)
HEADS_PER_GROUP = 8
ATT_WIDTH = N_DIL * HEADS_PER_GROUP * HEAD_DIM
ATT_OUT = HEADS_PER_GROUP * HEAD_DIM
ATT_BLK = 128
XA_HEADS = 4
SSD_HEAD_DIM = 64
SSD_STATE = 128
SSD_GROUPS = 8
SSD_CONV = 4
SSD_CHUNK = 128
CONV_ROW_STRIDE = 4
N_BRANCH = 2

D_MODEL = 1024
SSD_INNER = 2 * D_MODEL
SSD_HEADS = SSD_INNER // SSD_HEAD_DIM
SSD_CONV_CH = SSD_INNER + 2 * SSD_GROUPS * SSD_STATE
QKV_WIDTH = 3 * ATT_WIDTH
COL_Q, COL_K, COL_V = 0, ATT_WIDTH, 2 * ATT_WIDTH
SSD_Z0 = 0
SSD_XBC0 = SSD_Z0 + SSD_INNER
SSD_DT0 = SSD_XBC0 + SSD_CONV_CH
SSD_PROJ_WIDTH = SSD_DT0 + LANES


def _cparams(*sem):
    return pltpu.CompilerParams(dimension_semantics=sem, vmem_limit_bytes=VMEM_LIMIT)


def _resident(shape):
    nd = len(shape)
    return pl.BlockSpec(shape, lambda *_: (0,) * nd, pipeline_mode=pl.Buffered(1))


def _layer(stacked, l):
    tail = stacked.shape[1:]
    return pl.BlockSpec((None,) + tail, lambda *_: (l,) + (0,) * len(tail), pipeline_mode=pl.Buffered(1))


def _rms(x, g):
    return x * lax.rsqrt(jnp.mean(x * x, axis=-1, keepdims=True) + EPS) * g


def _silu(x):
    return x * (1.0 / (1.0 + jnp.exp(-x)))


def _dot(a, b):
    return jnp.dot(a, b, preferred_element_type=F32)


def _dot_nt(a, b):
    return lax.dot_general(a, b, (((1,), (1,)), ((), ())), preferred_element_type=F32)


def _ffn_kernel(x_ref, gin_ref, gout_ref, wg_ref, wu_ref, wd_ref, o_ref, *, fc):
    x = x_ref[...]
    xn = _rms(x, gin_ref[...]).astype(BF16)
    acc = None
    for c in range(wg_ref.shape[1] // fc):
        cols = slice(c * fc, (c + 1) * fc)
        a = (_silu(_dot(xn, wg_ref[:, cols])) * _dot(xn, wu_ref[:, cols])).astype(BF16)
        part = _dot(a, wd_ref[cols, :])
        acc = part if acc is None else acc + part
    o_ref[...] = x + 0.5 * _rms(acc, gout_ref[...])


def _ffn(h, g_in, g_out, wg, wu, wd, l, *, tm=512, fc=256):
    t, d = h.shape
    d_ff = wg.shape[2]
    assert d_ff % fc == 0
    return pl.pallas_call(
        functools.partial(_ffn_kernel, fc=fc),
        grid=(t // tm,),
        in_specs=[
            pl.BlockSpec((tm, d), lambda i: (i, 0)),
            _resident((1, d)), _resident((1, d)),
            _layer(wg, l), _layer(wu, l), _layer(wd, l),
        ],
        out_specs=pl.BlockSpec((tm, d), lambda i: (i, 0)),
        out_shape=jax.ShapeDtypeStruct((t, d), F32),
        compiler_params=_cparams("parallel"),
    )(h, g_in, g_out, wg, wu, wd)


def _proj_kernel(x_ref, g_ref, w_ref, o_ref, xn_ref):
    @pl.when(pl.program_id(1) == 0)
    def _():
        xn_ref[...] = _rms(x_ref[...], g_ref[...]).astype(BF16)

    o_ref[...] = _dot(xn_ref[...], w_ref[...])


def _proj(h, g, w, l, *, tm=2048, tn=768):
    t, d = h.shape
    n = w.shape[2]
    assert n % tn == 0
    return pl.pallas_call(
        _proj_kernel,
        grid=(t // tm, n // tn),
        in_specs=[
            pl.BlockSpec((tm, d), lambda i, j: (i, 0)),
            pl.BlockSpec((1, d), lambda i, j: (0, 0)),
            pl.BlockSpec((None, d, tn), lambda i, j: (l, 0, j)),
        ],
        out_specs=pl.BlockSpec((tm, tn), lambda i, j: (i, j)),
        out_shape=jax.ShapeDtypeStruct((t, n), F32),
        scratch_shapes=[pltpu.VMEM((tm, d), BF16)],
        compiler_params=_cparams("parallel", "arbitrary"),
    )(h, g, w)


def _att_group(q_ref, k_ref, v_ref, o_scr, l_scr, dil):
    seq = q_ref.shape[0]
    nb = seq // (dil * ATT_BLK)
    span = ATT_BLK * dil
    row = lax.broadcasted_iota(jnp.int32, (ATT_BLK, ATT_BLK), 0)
    col = lax.broadcasted_iota(jnp.int32, (ATT_BLK, ATT_BLK), 1)
    keep_prev = col >= row
    keep_cur = col <= row
    lane = lax.broadcasted_iota(jnp.int32, (ATT_BLK, LANES), 1)
    first_head = lane < HEAD_DIM
    scale = HEAD_DIM ** -0.5

    keep_both = jnp.concatenate([keep_prev, keep_cur], axis=1)

    def rows(start):
        return pl.ds(start, ATT_BLK, stride=dil) if dil > 1 else pl.ds(start, ATT_BLK)

    def residue(r):
        k_prev = v_prev = None
        for n in range(nb):
            cur = rows(r + n * span)
            q = q_ref[cur, :] * scale
            k_cur = k_ref[cur, :].astype(BF16)
            v_cur = v_ref[cur, :].astype(BF16)
            if n == 0:
                k_all, v_all, keep = k_cur, v_cur, keep_cur
            else:
                k_all = jnp.concatenate([k_prev, k_cur], axis=0)
                v_all = jnp.concatenate([v_prev, v_cur], axis=0)
                keep = keep_both
            k_prev, v_prev = k_cur, v_cur
            ms, dens, outs = [], [], []
            for head_sel in (first_head, jnp.logical_not(first_head)):
                qh = jnp.where(head_sel, q, 0.0).astype(BF16)
                s = jnp.where(keep, _dot_nt(qh, k_all), -jnp.inf)
                m = jnp.max(s, axis=-1, keepdims=True)
                p = jnp.exp(s - m)
                ms.append(m)
                dens.append(jnp.sum(p, axis=-1, keepdims=True))
                outs.append(_dot(p.astype(BF16), v_all))
            den = jnp.where(first_head, dens[0], dens[1])
            o_scr[cur, :] = jnp.where(first_head, outs[0], outs[1]) * (1.0 / den)
            l_scr[cur, :] = jnp.where(first_head, ms[0], ms[1]) + jnp.log(den)

    for r in range(dil):
        residue(r)


def _att_kernel(q_ref, k_ref, v_ref, y_ref, o_scr, l_scr):
    g = pl.program_id(2)
    for gi, (_, dil) in enumerate(DIL_GROUPS):
        @pl.when(g == gi)
        def _(gi=gi, dil=dil):
            _att_group(q_ref, k_ref, v_ref, o_scr.at[gi], l_scr.at[gi], dil)

    @pl.when(g == N_DIL - 1)
    def _():
        ls = [l_scr[gi] for gi in range(N_DIL)]
        m = functools.reduce(jnp.maximum, ls)
        es = [jnp.exp(l - m) for l in ls]
        inv = 1.0 / functools.reduce(lambda a, b: a + b, es)
        y = functools.reduce(lambda a, b: a + b, [(e * inv) * o_scr[gi] for gi, e in enumerate(es)])
        y_ref[...] = y.astype(y_ref.dtype)


def _dilated_attention(qkv3):
    b, s, _ = qkv3.shape
    pairs = ATT_OUT // LANES
    per_group = ATT_OUT // LANES

    def spec(col0):
        return pl.BlockSpec((None, s, LANES), lambda bi, p, g: (bi, 0, col0 // LANES + g * per_group + p))

    return pl.pallas_call(
        _att_kernel,
        grid=(b, pairs, N_DIL),
        in_specs=[spec(COL_Q), spec(COL_K), spec(COL_V)],
        out_specs=pl.BlockSpec((None, s, LANES), lambda bi, p, g: (bi, 0, p)),
        out_shape=jax.ShapeDtypeStruct((b, s, ATT_OUT), BF16),
        scratch_shapes=[pltpu.VMEM((N_DIL, s, LANES), F32), pltpu.VMEM((N_DIL, s, LANES), F32)],
        compiler_params=_cparams("parallel", "parallel", "arbitrary"),
    )(qkv3, qkv3, qkv3)


def _expand_heads(cols, h0):
    q = cols.shape[0]
    lane = lax.broadcasted_iota(jnp.int32, (q, LANES), 1)
    first = lane < SSD_HEAD_DIM
    halves = []
    for pair in range(2):
        a = jnp.broadcast_to(cols[:, h0 + 2 * pair:h0 + 2 * pair + 1], (q, LANES))
        b = jnp.broadcast_to(cols[:, h0 + 2 * pair + 1:h0 + 2 * pair + 2], (q, LANES))
        halves.append(jnp.where(first, a, b))
    return jnp.concatenate(halves, axis=1)


def _ssd_kernel(h_ref, gu_ref, w_ref, cw_ref, cb_ref, dtb_ref, alog_ref, dskip_ref, ng_ref, y_ref,
                xs_ref, xc_ref, st_ref):
    q = SSD_CHUNK
    u = _rms(h_ref[...], gu_ref[...]).astype(BF16)
    c = pl.program_id(1)
    n_slab = xs_ref.shape[0]
    tail = xs_ref.shape[1] - q

    @pl.when(c == 0)
    def _():
        xs_ref[:, 0:tail, :] = jnp.zeros((n_slab, tail, LANES), F32)
        st_ref[...] = jnp.zeros_like(st_ref)

    for j in range(n_slab):
        lanes = slice(j * LANES, (j + 1) * LANES)
        if j % 2 == 0:
            pre = _dot(u, w_ref[:, SSD_XBC0 + j * LANES:SSD_XBC0 + (j + 2) * LANES])
        xs_ref[j, tail:tail + q, :] = pre[:, (j % 2) * LANES:(j % 2 + 1) * LANES]
        for e in range(CONV_ROW_STRIDE):
            acc = cb_ref[:, lanes]
            for kk in range(SSD_CONV):
                start = tail + e - (SSD_CONV - 1) + kk
                acc = acc + cw_ref[kk:kk + 1, lanes] * xs_ref[j, pl.ds(start, q // CONV_ROW_STRIDE,
                                                                       stride=CONV_ROW_STRIDE), :]
            xc_ref[j, pl.ds(e, q // CONV_ROW_STRIDE, stride=CONV_ROW_STRIDE), :] = _silu(acc)
        xs_ref[j, 0:tail, :] = xs_ref[j, q:q + tail, :]

    dtv = _dot(u, w_ref[:, SSD_DT0:SSD_DT0 + LANES]) + dtb_ref[...]
    dtv = jnp.maximum(dtv, 0.0) + jnp.log1p(jnp.exp(-jnp.abs(dtv)))
    a_cs = dtv * (-jnp.exp(alog_ref[...]))
    rowi = lax.broadcasted_iota(jnp.int32, (q, LANES), 0)
    shift = 1
    while shift < q:
        a_cs = a_cs + jnp.where(rowi >= shift, pltpu.roll(a_cs, shift, axis=0), 0.0)
        shift *= 2
    a_cs = a_cs * LOG2E
    a_cs_t = a_cs.T

    li = lax.broadcasted_iota(jnp.int32, (q, q), 0)
    si = lax.broadcasted_iota(jnp.int32, (q, q), 1)
    causal = li >= si
    lane = lax.broadcasted_iota(jnp.int32, (q, LANES), 1)
    first = lane < SSD_HEAD_DIM
    heads_per_group = SSD_HEADS // SSD_GROUPS
    gw = heads_per_group * SSD_HEAD_DIM
    b0 = SSD_INNER
    c0 = SSD_INNER + SSD_GROUPS * SSD_STATE

    for g in range(SSD_GROUPS):
        h0 = g * heads_per_group
        xg = jnp.concatenate([xc_ref[g * gw // LANES + i] for i in range(gw // LANES)], axis=1)
        bg = xc_ref[b0 // LANES + g]
        cg = xc_ref[c0 // LANES + g].astype(BF16)
        dt_e = _expand_heads(dtv, h0)
        acs_e = _expand_heads(a_cs, h0)
        last_e = acs_e[q - 1:q, :]
        xdt = xg * dt_e
        cb = _dot_nt(cg, bg.astype(BF16))

        y_halves = []
        for pair in range(2):
            xdt_pair = xdt[:, pair * LANES:(pair + 1) * LANES].astype(BF16)
            ys = []
            for j in range(2):
                hh = h0 + 2 * pair + j
                seg = a_cs[:, hh:hh + 1] - a_cs_t[hh:hh + 1, :]
                w = (cb * jnp.exp2(jnp.where(causal, seg, -jnp.inf))).astype(BF16)
                ys.append(_dot(w, xdt_pair))
            y_halves.append(jnp.where(first, ys[0], ys[1]))
        y = jnp.concatenate(y_halves, axis=1)

        prev = st_ref[g]
        y = y + _dot(cg, prev.astype(BF16)) * jnp.exp2(acs_e)
        upd = _dot(bg.T.astype(BF16), (xdt * jnp.exp2(last_e - acs_e)).astype(BF16))
        st_ref[g] = prev * jnp.exp2(last_e) + upd

        y = y + dskip_ref[:, g * gw:(g + 1) * gw] * xg
        y = y * _silu(_dot(u, w_ref[:, SSD_Z0 + g * gw:SSD_Z0 + (g + 1) * gw]))
        y = y * lax.rsqrt(jnp.mean(y * y, axis=-1, keepdims=True) + EPS)
        y_ref[:, g * gw:(g + 1) * gw] = (y * ng_ref[:, g * gw:(g + 1) * gw]).astype(y_ref.dtype)


def _ssd(h3, g_u, w, l, conv_w, conv_b, dt_bias, a_log, d_skip, norm_g):
    b, s, d = h3.shape
    q = SSD_CHUNK
    tail = 8
    return pl.pallas_call(
        _ssd_kernel,
        grid=(b, s // q),
        in_specs=[
            pl.BlockSpec((None, q, d), lambda bi, c: (bi, c, 0)),
            _resident((1, d)), _layer(w, l),
            _resident((SSD_CONV, SSD_CONV_CH)), _resident((1, SSD_CONV_CH)),
            _resident((1, LANES)), _resident((1, LANES)),
            _resident((1, SSD_INNER)), _resident((1, SSD_INNER)),
        ],
        out_specs=pl.BlockSpec((None, q, SSD_INNER), lambda bi, c: (bi, c, 0)),
        out_shape=jax.ShapeDtypeStruct((b, s, SSD_INNER), BF16),
        scratch_shapes=[
            pltpu.VMEM((SSD_CONV_CH // LANES, tail + q, LANES), F32),
            pltpu.VMEM((SSD_CONV_CH // LANES, q, LANES), F32),
            pltpu.VMEM((SSD_GROUPS, SSD_STATE, SSD_INNER // SSD_GROUPS), F32),
        ],
        compiler_params=_cparams("parallel", "arbitrary"),
    )(h3, g_u, w, conv_w, conv_b, dt_bias, a_log, d_skip, norm_g)


def _merge_kernel(h_ref, ya_ref, ys_ref, gu_ref, wgate_ref, bg_ref, wa_ref, ws_ref, wo_ref, gn_ref, o_ref):
    h = h_ref[...]
    d = h.shape[1]
    gates = _dot(_rms(h, gu_ref[...]).astype(BF16), wgate_ref[...]) + bg_ref[...]
    gates = 1.0 / (1.0 + jnp.exp(-gates))
    merged = gates[:, 0:d] * _dot(ya_ref[...], wa_ref[...]) + gates[:, d:2 * d] * _dot(ys_ref[...], ws_ref[...])
    o_ref[...] = h + _rms(_dot(merged.astype(BF16), wo_ref[...]), gn_ref[...])


def _merge(h, y_att, y_ssd, g_u, w_gate, b_gate, wa, ws, wo, gn, l, *, tm=512):
    t, d = h.shape
    return pl.pallas_call(
        _merge_kernel,
        grid=(t // tm,),
        in_specs=[
            pl.BlockSpec((tm, d), lambda i: (i, 0)),
            pl.BlockSpec((tm, ATT_OUT), lambda i: (i, 0)),
            pl.BlockSpec((tm, SSD_INNER), lambda i: (i, 0)),
            _resident((1, d)), _layer(w_gate, l), _resident((1, N_BRANCH * d)),
            _layer(wa, l), _layer(ws, l), _layer(wo, l),
            _resident((1, d)),
        ],
        out_specs=pl.BlockSpec((tm, d), lambda i: (i, 0)),
        out_shape=jax.ShapeDtypeStruct((t, d), F32),
        compiler_params=_cparams("parallel"),
    )(h, y_att, y_ssd, g_u, w_gate, b_gate, wa, ws, wo, gn)


def _memkv_kernel(mem_ref, g_ref, wk_ref, wv_ref, k_ref, v_ref):
    mn = _rms(mem_ref[...], g_ref[...]).astype(BF16)
    k_ref[...] = _dot(mn, wk_ref[...]).astype(k_ref.dtype)
    v_ref[...] = _dot(mn, wv_ref[...]).astype(v_ref.dtype)


def _memkv(mem, g, wk, wv, l):
    b, m, d = mem.shape
    blk = pl.BlockSpec((None, m, d), lambda bi: (bi, 0, 0))
    return pl.pallas_call(
        _memkv_kernel,
        grid=(b,),
        in_specs=[blk, _resident((1, d)), _layer(wk, l), _layer(wv, l)],
        out_specs=[blk, blk],
        out_shape=[jax.ShapeDtypeStruct((b, m, d), BF16)] * 2,
        compiler_params=_cparams("parallel"),
    )(mem, g, wk, wv)


def _xattn_kernel(h_ref, k_ref, v_ref, gin_ref, gout_ref, wq_ref, wo_ref, o_ref):
    h = h_ref[...]
    d = h.shape[1]
    hd = d // XA_HEADS
    q = _dot(_rms(h, gin_ref[...]).astype(BF16), wq_ref[...]).astype(BF16)
    outs = []
    for hh in range(XA_HEADS):
        cols = slice(hh * hd, (hh + 1) * hd)
        s = _dot_nt(q[:, cols], k_ref[:, cols]) * (hd ** -0.5)
        p = jnp.exp(s - jnp.max(s, axis=-1, keepdims=True))
        p = p * (1.0 / jnp.sum(p, axis=-1, keepdims=True))
        outs.append(_dot(p.astype(BF16), v_ref[:, cols]))
    o = jnp.concatenate(outs, axis=1).astype(BF16)
    o_ref[...] = h + _rms(_dot(o, wo_ref[...]), gout_ref[...])


def _xattn(h3, k, v, g_in, g_out, wq, wo, l, *, tm=512):
    b, s, d = h3.shape
    m = k.shape[1]
    tok = pl.BlockSpec((None, tm, d), lambda bi, i: (bi, i, 0))
    kv = pl.BlockSpec((None, m, d), lambda bi, i: (bi, 0, 0))
    return pl.pallas_call(
        _xattn_kernel,
        grid=(b, s // tm),
        in_specs=[tok, kv, kv, _resident((1, d)), _resident((1, d)), _layer(wq, l), _layer(wo, l)],
        out_specs=tok,
        out_shape=jax.ShapeDtypeStruct((b, s, d), F32),
        compiler_params=_cparams("parallel", "parallel"),
    )(h3, k, v, g_in, g_out, wq, wo)


def _split_w_in(w):
    o_z = QKV_WIDTH
    o_gate = o_z + SSD_INNER + SSD_CONV_CH + SSD_HEADS
    w_ssd = jnp.pad(w[:, :, o_z:o_gate], ((0, 0), (0, 0), (0, SSD_PROJ_WIDTH - (o_gate - o_z))))
    return w[:, :, :o_z], w_ssd, w[:, :, o_gate:]


def _pad_lanes(v):
    return jnp.pad(v, (0, LANES - v.shape[0])).reshape(1, LANES)


def kernel(x, mem, norm_g, ffn1_gate, ffn1_up, ffn1_down, w_in, b_gate, conv_w, conv_b, dt_bias, a_log, d_skip,
           ssd_norm_g, w_att_out, w_ssd_out, w_o, mem_norm_g, xa_wq, xa_wk, xa_wv, xa_wo, ffn2_gate, ffn2_up,
           ffn2_down):
    b, s, d = x.shape
    depth = norm_g.shape[0]
    assert d == D_MODEL and s % (DIL_GROUPS[-1][1] * ATT_BLK) == 0
    bf = lambda w: w.astype(BF16)
    ffn1_gate, ffn1_up, ffn1_down = bf(ffn1_gate), bf(ffn1_up), bf(ffn1_down)
    ffn2_gate, ffn2_up, ffn2_down = bf(ffn2_gate), bf(ffn2_up), bf(ffn2_down)
    w_qkv, w_ssd_in, w_gate = _split_w_in(bf(w_in))
    w_att_out, w_ssd_out, w_o = bf(w_att_out), bf(w_ssd_out), bf(w_o)
    xa_wq, xa_wk, xa_wv, xa_wo = bf(xa_wq), bf(xa_wk), bf(xa_wv), bf(xa_wo)

    h = x.reshape(b * s, d)
    for l in range(depth):
        g = norm_g[l].reshape(-1, 1, d)
        h = _ffn(h, g[0], g[1], ffn1_gate, ffn1_up, ffn1_down, l)
        qkv3 = _proj(h, g[2], w_qkv, l).reshape(b, s, QKV_WIDTH)
        y_att = _dilated_attention(qkv3).reshape(b * s, ATT_OUT)
        y_ssd = _ssd(h.reshape(b, s, d), g[2], w_ssd_in, l, conv_w[l], conv_b[l].reshape(1, -1),
                     _pad_lanes(dt_bias[l]), _pad_lanes(a_log[l]),
                     jnp.repeat(d_skip[l], SSD_HEAD_DIM).reshape(1, -1), ssd_norm_g[l].reshape(1, -1))
        h = _merge(h, y_att, y_ssd.reshape(b * s, SSD_INNER), g[2], w_gate, b_gate[l].reshape(1, -1),
                   w_att_out, w_ssd_out, w_o, g[3], l)
        k_mem, v_mem = _memkv(mem, mem_norm_g[l].reshape(1, d), xa_wk, xa_wv, l)
        h = _xattn(h.reshape(b, s, d), k_mem, v_mem, g[4], g[5], xa_wq, xa_wo, l).reshape(b * s, d)
        h = _ffn(h, g[6], g[7], ffn2_gate, ffn2_up, ffn2_down, l)
    return h.reshape(b, s, d)
```

```python
import functools

import jax
import jax.numpy as jnp
from jax import lax
from jax.experimental import pallas as pl
from jax.experimental.pallas import tpu as pltpu

F32 = jnp.float32
BF16 = jnp.bfloat16

EPS = 1e-6
LOG2E = 1.4426950408889634
LANES = 128
VMEM_LIMIT = 56 * 1024 * 1024

HEAD_DIM = 64
DIL_GROUPS = ((128, 1), (512, 4), (2048, 16))
N_DIL = len(DIL_GROUPS)
HEADS_PER_GROUP = 8
ATT_WIDTH = N_DIL * HEADS_PER_GROUP * HEAD_DIM
ATT_OUT = HEADS_PER_GROUP * HEAD_DIM
ATT_BLK = 128
XA_HEADS = 4
SSD_HEAD_DIM = 64
SSD_STATE = 128
SSD_GROUPS = 8
SSD_CONV = 4
SSD_CHUNK = 128
SSD_STEP_CHUNKS = 2
CONV_ROW_STRIDE = 4
N_BRANCH = 2

D_MODEL = 1024
SSD_INNER = 2 * D_MODEL
SSD_HEADS = SSD_INNER // SSD_HEAD_DIM
SSD_CONV_CH = SSD_INNER + 2 * SSD_GROUPS * SSD_STATE
QKV_WIDTH = 3 * ATT_WIDTH
COL_Q, COL_K, COL_V = 0, ATT_WIDTH, 2 * ATT_WIDTH
CONV_PAIR = 2 * LANES


def _cparams(*sem):
    return pltpu.CompilerParams(dimension_semantics=sem, vmem_limit_bytes=VMEM_LIMIT)


def _resident(shape):
    nd = len(shape)
    return pl.BlockSpec(shape, lambda *_: (0,) * nd, pipeline_mode=pl.Buffered(1))


def _layer(stacked, l):
    tail = stacked.shape[1:]
    return pl.BlockSpec((None,) + tail, lambda *_: (l,) + (0,) * len(tail), pipeline_mode=pl.Buffered(1))


def _rms(x, g):
    return x * lax.rsqrt(jnp.mean(x * x, axis=-1, keepdims=True) + EPS) * g


def _silu(x):
    return x * (1.0 / (1.0 + jnp.exp(-x)))


def _dot(a, b):
    return jnp.dot(a, b, preferred_element_type=F32)


def _dot_nt(a, b):
    return lax.dot_general(a, b, (((1,), (1,)), ((), ())), preferred_element_type=F32)


def _ffn_kernel(x_ref, gin_ref, gout_ref, wg_ref, wu_ref, wd_ref, o_ref, *, fc, sub):
    for r0 in range(0, x_ref.shape[0], sub):
        x = x_ref[r0:r0 + sub, :]
        xn = _rms(x, gin_ref[...]).astype(BF16)
        acc = None
        for c in range(wg_ref.shape[1] // fc):
            cols = slice(c * fc, (c + 1) * fc)
            a = (_silu(_dot(xn, wg_ref[:, cols])) * _dot(xn, wu_ref[:, cols])).astype(BF16)
            part = _dot(a, wd_ref[cols, :])
            acc = part if acc is None else acc + part
        o_ref[r0:r0 + sub, :] = x + 0.5 * _rms(acc, gout_ref[...])


def _ffn(h, g_in, g_out, wg, wu, wd, l, *, tm=1024, sub=512, fc=256):
    t, d = h.shape
    d_ff = wg.shape[2]
    assert d_ff % fc == 0 and tm % sub == 0
    return pl.pallas_call(
        functools.partial(_ffn_kernel, fc=fc, sub=sub),
        grid=(t // tm,),
        in_specs=[
            pl.BlockSpec((tm, d), lambda i: (i, 0)),
            _resident((1, d)), _resident((1, d)),
            _layer(wg, l), _layer(wu, l), _layer(wd, l),
        ],
        out_specs=pl.BlockSpec((tm, d), lambda i: (i, 0)),
        out_shape=jax.ShapeDtypeStruct((t, d), F32),
        compiler_params=_cparams("parallel"),
    )(h, g_in, g_out, wg, wu, wd)


def _proj_kernel(x_ref, g_ref, w_ref, o_ref, xn_ref):
    @pl.when(pl.program_id(1) == 0)
    def _():
        xn_ref[...] = _rms(x_ref[...], g_ref[...]).astype(BF16)

    o_ref[...] = _dot(xn_ref[...], w_ref[...])


def _proj(h, g, w, l, *, tm=1024, tn=1536):
    t, d = h.shape
    n = w.shape[2]
    assert n % tn == 0
    return pl.pallas_call(
        _proj_kernel,
        grid=(t // tm, n // tn),
        in_specs=[
            pl.BlockSpec((tm, d), lambda i, j: (i, 0)),
            pl.BlockSpec((1, d), lambda i, j: (0, 0)),
            pl.BlockSpec((None, d, tn), lambda i, j: (l, 0, j)),
        ],
        out_specs=pl.BlockSpec((tm, tn), lambda i, j: (i, j)),
        out_shape=jax.ShapeDtypeStruct((t, n), F32),
        scratch_shapes=[pltpu.VMEM((tm, d), BF16)],
        compiler_params=_cparams("parallel", "arbitrary"),
    )(h, g, w)


def _att_group(q_ref, k_ref, v_ref, o_scr, l_scr, dil):
    seq = q_ref.shape[0]
    nb = seq // (dil * ATT_BLK)
    span = ATT_BLK * dil
    row = lax.broadcasted_iota(jnp.int32, (ATT_BLK, ATT_BLK), 0)
    col = lax.broadcasted_iota(jnp.int32, (ATT_BLK, ATT_BLK), 1)
    keep_prev = col >= row
    keep_cur = col <= row
    lane = lax.broadcasted_iota(jnp.int32, (ATT_BLK, LANES), 1)
    first_head = lane < HEAD_DIM
    scale = HEAD_DIM ** -0.5

    keep_both = jnp.concatenate([keep_prev, keep_cur], axis=1)

    def rows(start):
        return pl.ds(start, ATT_BLK, stride=dil) if dil > 1 else pl.ds(start, ATT_BLK)

    def residue(r):
        k_prev = v_prev = None
        for n in range(nb):
            cur = rows(r + n * span)
            q = q_ref[cur, :] * scale
            k_cur = k_ref[cur, :].astype(BF16)
            v_cur = v_ref[cur, :].astype(BF16)
            if n == 0:
                k_all, v_all, keep = k_cur, v_cur, keep_cur
            else:
                k_all = jnp.concatenate([k_prev, k_cur], axis=0)
                v_all = jnp.concatenate([v_prev, v_cur], axis=0)
                keep = keep_both
            k_prev, v_prev = k_cur, v_cur
            ms, dens, outs = [], [], []
            for head_sel in (first_head, jnp.logical_not(first_head)):
                qh = jnp.where(head_sel, q, 0.0).astype(BF16)
                s = jnp.where(keep, _dot_nt(qh, k_all), -jnp.inf)
                m = jnp.max(s, axis=-1, keepdims=True)
                p = jnp.exp(s - m)
                ms.append(m)
                dens.append(jnp.sum(p, axis=-1, keepdims=True))
                outs.append(_dot(p.astype(BF16), v_all))
            den = jnp.where(first_head, dens[0], dens[1])
            o_scr[cur, :] = jnp.where(first_head, outs[0], outs[1]) * (1.0 / den)
            l_scr[cur, :] = jnp.where(first_head, ms[0], ms[1]) + jnp.log(den)

    for r in range(dil):
        residue(r)


def _att_kernel(q_ref, k_ref, v_ref, y_ref, o_scr, l_scr):
    g = pl.program_id(2)
    for gi, (_, dil) in enumerate(DIL_GROUPS):
        @pl.when(g == gi)
        def _(gi=gi, dil=dil):
            _att_group(q_ref, k_ref, v_ref, o_scr.at[gi], l_scr.at[gi], dil)

    @pl.when(g == N_DIL - 1)
    def _():
        ls = [l_scr[gi] for gi in range(N_DIL)]
        m = functools.reduce(jnp.maximum, ls)
        es = [jnp.exp(l - m) for l in ls]
        inv = 1.0 / functools.reduce(lambda a, b: a + b, es)
        y = functools.reduce(lambda a, b: a + b, [(e * inv) * o_scr[gi] for gi, e in enumerate(es)])
        y_ref[...] = y.astype(y_ref.dtype)


def _dilated_attention(qkv3):
    b, s, _ = qkv3.shape
    pairs = ATT_OUT // LANES
    per_group = ATT_OUT // LANES

    def spec(col0):
        return pl.BlockSpec((None, s, LANES), lambda bi, p, g: (bi, 0, col0 // LANES + g * per_group + p))

    return pl.pallas_call(
        _att_kernel,
        grid=(b, pairs, N_DIL),
        in_specs=[spec(COL_Q), spec(COL_K), spec(COL_V)],
        out_specs=pl.BlockSpec((None, s, LANES), lambda bi, p, g: (bi, 0, p)),
        out_shape=jax.ShapeDtypeStruct((b, s, ATT_OUT), BF16),
        scratch_shapes=[pltpu.VMEM((N_DIL, s, LANES), F32), pltpu.VMEM((N_DIL, s, LANES), F32)],
        compiler_params=_cparams("parallel", "parallel", "arbitrary"),
    )(qkv3, qkv3, qkv3)


def _expand_heads(cols, h0):
    q = cols.shape[0]
    lane = lax.broadcasted_iota(jnp.int32, (q, LANES), 1)
    first = lane < SSD_HEAD_DIM
    halves = []
    for pair in range(2):
        a = jnp.broadcast_to(cols[:, h0 + 2 * pair:h0 + 2 * pair + 1], (q, LANES))
        b = jnp.broadcast_to(cols[:, h0 + 2 * pair + 1:h0 + 2 * pair + 2], (q, LANES))
        halves.append(jnp.where(first, a, b))
    return jnp.concatenate(halves, axis=1)


def _ssd_kernel(h_ref, gu_ref, wx_ref, wz_ref, wdt_ref, cw_ref, cb_ref, dtb_ref, alog_ref, dskip_ref, ng_ref,
                y_ref, xs_ref, xc_ref, st_ref):
    q = SSD_CHUNK
    rows = h_ref.shape[0]
    n_slab = xs_ref.shape[0]
    tail = xs_ref.shape[1] - rows
    u = _rms(h_ref[...], gu_ref[...]).astype(BF16)

    @pl.when(pl.program_id(1) == 0)
    def _():
        xs_ref[:, 0:tail, :] = jnp.zeros((n_slab, tail, LANES), F32)
        st_ref[...] = jnp.zeros_like(st_ref)

    slabs_per_pair = CONV_PAIR // LANES
    per_class = rows // CONV_ROW_STRIDE
    for p in range(n_slab // slabs_per_pair):
        pre = _dot(u, wx_ref[p])
        for i in range(slabs_per_pair):
            j = slabs_per_pair * p + i
            xs_ref[j, tail:tail + rows, :] = pre[:, i * LANES:(i + 1) * LANES]
            for e in range(CONV_ROW_STRIDE):
                acc = cb_ref[j]
                for kk in range(SSD_CONV):
                    start = tail + e - (SSD_CONV - 1) + kk
                    acc = acc + cw_ref[j, kk:kk + 1, :] * xs_ref[j, pl.ds(start, per_class, stride=CONV_ROW_STRIDE), :]
                xc_ref[j, pl.ds(e, per_class, stride=CONV_ROW_STRIDE), :] = _silu(acc)
            xs_ref[j, 0:tail, :] = xs_ref[j, rows:rows + tail, :]

    dt_all = _dot(u, wdt_ref[...]) + dtb_ref[...]
    dt_all = jnp.maximum(dt_all, 0.0) + jnp.log1p(jnp.exp(-jnp.abs(dt_all)))
    neg_a = -jnp.exp(alog_ref[...])
    rowi = lax.broadcasted_iota(jnp.int32, (q, LANES), 0)
    dts, a_css, a_cs_ts = [], [], []
    for ch in range(rows // q):
        dtv = dt_all[ch * q:(ch + 1) * q, :]
        a_cs = dtv * neg_a
        shift = 1
        while shift < q:
            a_cs = a_cs + jnp.where(rowi >= shift, pltpu.roll(a_cs, shift, axis=0), 0.0)
            shift *= 2
        a_cs = a_cs * LOG2E
        dts.append(dtv)
        a_css.append(a_cs)
        a_cs_ts.append(a_cs.T)

    li = lax.broadcasted_iota(jnp.int32, (q, q), 0)
    si = lax.broadcasted_iota(jnp.int32, (q, q), 1)
    causal = li >= si
    lane = lax.broadcasted_iota(jnp.int32, (q, LANES), 1)
    first = lane < SSD_HEAD_DIM
    heads_per_group = SSD_HEADS // SSD_GROUPS
    gw = heads_per_group * SSD_HEAD_DIM
    b0 = SSD_INNER
    c0 = SSD_INNER + SSD_GROUPS * SSD_STATE

    for g in range(SSD_GROUPS):
        h0 = g * heads_per_group
        gate = _silu(_dot(u, wz_ref[:, g * gw:(g + 1) * gw]))
        for ch in range(rows // q):
            r0 = ch * q
            dtv, a_cs, a_cs_t = dts[ch], a_css[ch], a_cs_ts[ch]
            xg = jnp.concatenate([xc_ref[g * gw // LANES + i, r0:r0 + q, :] for i in range(gw // LANES)], axis=1)
            bg = xc_ref[b0 // LANES + g, r0:r0 + q, :]
            cg = xc_ref[c0 // LANES + g, r0:r0 + q, :].astype(BF16)
            dt_e = _expand_heads(dtv, h0)
            acs_e = _expand_heads(a_cs, h0)
            last_e = acs_e[q - 1:q, :]
            xdt = xg * dt_e
            cb = _dot_nt(cg, bg.astype(BF16))

            y_halves = []
            for pair in range(2):
                xdt_pair = xdt[:, pair * LANES:(pair + 1) * LANES].astype(BF16)
                ys = []
                for j in range(2):
                    hh = h0 + 2 * pair + j
                    seg = a_cs[:, hh:hh + 1] - a_cs_t[hh:hh + 1, :]
                    w = (cb * jnp.exp2(jnp.where(causal, seg, -jnp.inf))).astype(BF16)
                    ys.append(_dot(w, xdt_pair))
                y_halves.append(jnp.where(first, ys[0], ys[1]))
            y = jnp.concatenate(y_halves, axis=1)

            prev = st_ref[g]
            y = y + _dot(cg, prev.astype(BF16)) * jnp.exp2(acs_e)
            upd = _dot(bg.T.astype(BF16), (xdt * jnp.exp2(last_e - acs_e)).astype(BF16))
            st_ref[g] = prev * jnp.exp2(last_e) + upd

            y = y + dskip_ref[:, g * gw:(g + 1) * gw] * xg
            y = y * gate[r0:r0 + q, :]
            y = y * lax.rsqrt(jnp.mean(y * y, axis=-1, keepdims=True) + EPS)
            y_ref[r0:r0 + q, g * gw:(g + 1) * gw] = (y * ng_ref[:, g * gw:(g + 1) * gw]).astype(y_ref.dtype)


def _ssd(h3, g_u, wx, wz, wdt, l, conv_w, conv_b, dt_bias, a_log, d_skip, norm_g):
    b, s, d = h3.shape
    rows = SSD_STEP_CHUNKS * SSD_CHUNK
    tail = 8
    assert s % rows == 0 and rows % CONV_ROW_STRIDE == 0
    return pl.pallas_call(
        _ssd_kernel,
        grid=(b, s // rows),
        in_specs=[
            pl.BlockSpec((None, rows, d), lambda bi, c: (bi, c, 0)),
            _resident((1, d)), _layer(wx, l), _layer(wz, l), _layer(wdt, l),
            _resident(conv_w.shape), _resident(conv_b.shape),
            _resident((1, LANES)), _resident((1, LANES)),
            _resident((1, SSD_INNER)), _resident((1, SSD_INNER)),
        ],
        out_specs=pl.BlockSpec((None, rows, SSD_INNER), lambda bi, c: (bi, c, 0)),
        out_shape=jax.ShapeDtypeStruct((b, s, SSD_INNER), BF16),
        scratch_shapes=[
            pltpu.VMEM((SSD_CONV_CH // LANES, tail + rows, LANES), F32),
            pltpu.VMEM((SSD_CONV_CH // LANES, rows, LANES), F32),
            pltpu.VMEM((SSD_GROUPS, SSD_STATE, SSD_INNER // SSD_GROUPS), F32),
        ],
        compiler_params=_cparams("parallel", "arbitrary"),
    )(h3, g_u, wx, wz, wdt, conv_w, conv_b, dt_bias, a_log, d_skip, norm_g)


def _merge_kernel(h_ref, ya_ref, ys_ref, gu_ref, wgate_ref, bg_ref, wa_ref, ws_ref, wo_ref, gn_ref, o_ref):
    h = h_ref[...]
    d = h.shape[1]
    gates = _dot(_rms(h, gu_ref[...]).astype(BF16), wgate_ref[...]) + bg_ref[...]
    gates = 1.0 / (1.0 + jnp.exp(-gates))
    merged = gates[:, 0:d] * _dot(ya_ref[...], wa_ref[...]) + gates[:, d:2 * d] * _dot(ys_ref[...], ws_ref[...])
    o_ref[...] = h + _rms(_dot(merged.astype(BF16), wo_ref[...]), gn_ref[...])


def _merge(h, y_att, y_ssd, g_u, w_gate, b_gate, wa, ws, wo, gn, l, *, tm=512):
    t, d = h.shape
    return pl.pallas_call(
        _merge_kernel,
        grid=(t // tm,),
        in_specs=[
            pl.BlockSpec((tm, d), lambda i: (i, 0)),
            pl.BlockSpec((tm, ATT_OUT), lambda i: (i, 0)),
            pl.BlockSpec((tm, SSD_INNER), lambda i: (i, 0)),
            _resident((1, d)), _layer(w_gate, l), _resident((1, N_BRANCH * d)),
            _layer(wa, l), _layer(ws, l), _layer(wo, l),
            _resident((1, d)),
        ],
        out_specs=pl.BlockSpec((tm, d), lambda i: (i, 0)),
        out_shape=jax.ShapeDtypeStruct((t, d), F32),
        compiler_params=_cparams("parallel"),
    )(h, y_att, y_ssd, g_u, w_gate, b_gate, wa, ws, wo, gn)


def _memkv_kernel(mem_ref, g_ref, wk_ref, wv_ref, k_ref, v_ref):
    mn = _rms(mem_ref[...], g_ref[...]).astype(BF16)
    k_ref[...] = _dot(mn, wk_ref[...]).astype(k_ref.dtype)
    v_ref[...] = _dot(mn, wv_ref[...]).astype(v_ref.dtype)


def _memkv(mem, g, wk, wv, l):
    b, m, d = mem.shape
    blk = pl.BlockSpec((None, m, d), lambda bi: (bi, 0, 0))
    return pl.pallas_call(
        _memkv_kernel,
        grid=(b,),
        in_specs=[blk, _resident((1, d)), _layer(wk, l), _layer(wv, l)],
        out_specs=[blk, blk],
        out_shape=[jax.ShapeDtypeStruct((b, m, d), BF16)] * 2,
        compiler_params=_cparams("parallel"),
    )(mem, g, wk, wv)


def _xattn_kernel(h_ref, k_ref, v_ref, gin_ref, gout_ref, wq_ref, wo_ref, o_ref):
    h = h_ref[...]
    d = h.shape[1]
    hd = d // XA_HEADS
    q = _dot(_rms(h, gin_ref[...]).astype(BF16), wq_ref[...]).astype(BF16)
    outs = []
    for hh in range(XA_HEADS):
        cols = slice(hh * hd, (hh + 1) * hd)
        s = _dot_nt(q[:, cols], k_ref[:, cols]) * (hd ** -0.5)
        p = jnp.exp(s - jnp.max(s, axis=-1, keepdims=True))
        p = p * (1.0 / jnp.sum(p, axis=-1, keepdims=True))
        outs.append(_dot(p.astype(BF16), v_ref[:, cols]))
    o = jnp.concatenate(outs, axis=1).astype(BF16)
    o_ref[...] = h + _rms(_dot(o, wo_ref[...]), gout_ref[...])


def _xattn(h3, k, v, g_in, g_out, wq, wo, l, *, tm=1024):
    b, s, d = h3.shape
    m = k.shape[1]
    tok = pl.BlockSpec((None, tm, d), lambda bi, i: (bi, i, 0))
    kv = pl.BlockSpec((None, m, d), lambda bi, i: (bi, 0, 0))
    return pl.pallas_call(
        _xattn_kernel,
        grid=(b, s // tm),
        in_specs=[tok, kv, kv, _resident((1, d)), _resident((1, d)), _layer(wq, l), _layer(wo, l)],
        out_specs=tok,
        out_shape=jax.ShapeDtypeStruct((b, s, d), F32),
        compiler_params=_cparams("parallel", "parallel"),
    )(h3, k, v, g_in, g_out, wq, wo)


def _split_w_in(w):
    depth, d, _ = w.shape
    o_z = QKV_WIDTH
    o_xbc = o_z + SSD_INNER
    o_dt = o_xbc + SSD_CONV_CH
    o_gate = o_dt + SSD_HEADS
    bf = lambda a: a.astype(BF16)
    wx = bf(w[:, :, o_xbc:o_dt]).reshape(depth, d, SSD_CONV_CH // CONV_PAIR, CONV_PAIR).transpose(0, 2, 1, 3)
    wdt = jnp.pad(bf(w[:, :, o_dt:o_gate]), ((0, 0), (0, 0), (0, LANES - SSD_HEADS)))
    return bf(w[:, :, :o_z]), bf(w[:, :, o_z:o_xbc]), wx, wdt, bf(w[:, :, o_gate:])


def _slabs(a):
    r, c = a.shape
    return a.reshape(r, c // LANES, LANES).transpose(1, 0, 2)


def _pad_lanes(v):
    return jnp.pad(v, (0, LANES - v.shape[0])).reshape(1, LANES)


def kernel(x, mem, norm_g, ffn1_gate, ffn1_up, ffn1_down, w_in, b_gate, conv_w, conv_b, dt_bias, a_log, d_skip,
           ssd_norm_g, w_att_out, w_ssd_out, w_o, mem_norm_g, xa_wq, xa_wk, xa_wv, xa_wo, ffn2_gate, ffn2_up,
           ffn2_down):
    b, s, d = x.shape
    depth = norm_g.shape[0]
    assert d == D_MODEL and s % (DIL_GROUPS[-1][1] * ATT_BLK) == 0
    bf = lambda w: w.astype(BF16)
    ffn1_gate, ffn1_up, ffn1_down = bf(ffn1_gate), bf(ffn1_up), bf(ffn1_down)
    ffn2_gate, ffn2_up, ffn2_down = bf(ffn2_gate), bf(ffn2_up), bf(ffn2_down)
    w_qkv, w_z, w_xbc, w_dt, w_gate = _split_w_in(w_in)
    w_att_out, w_ssd_out, w_o = bf(w_att_out), bf(w_ssd_out), bf(w_o)
    xa_wq, xa_wk, xa_wv, xa_wo = bf(xa_wq), bf(xa_wk), bf(xa_wv), bf(xa_wo)

    h = x.reshape(b * s, d)
    for l in range(depth):
        g = norm_g[l].reshape(-1, 1, d)
        h = _ffn(h, g[0], g[1], ffn1_gate, ffn1_up, ffn1_down, l)
        qkv3 = _proj(h, g[2], w_qkv, l).reshape(b, s, QKV_WIDTH)
        y_att = _dilated_attention(qkv3).reshape(b * s, ATT_OUT)
        y_ssd = _ssd(h.reshape(b, s, d), g[2], w_xbc, w_z, w_dt, l, _slabs(conv_w[l]), _slabs(conv_b[l].reshape(1, -1)),
                     _pad_lanes(dt_bias[l]), _pad_lanes(a_log[l]),
                     jnp.repeat(d_skip[l], SSD_HEAD_DIM).reshape(1, -1), ssd_norm_g[l].reshape(1, -1))
        h = _merge(h, y_att, y_ssd.reshape(b * s, SSD_INNER), g[2], w_gate, b_gate[l].reshape(1, -1),
                   w_att_out, w_ssd_out, w_o, g[3], l)
        k_mem, v_mem = _memkv(mem, mem_norm_g[l].reshape(1, d), xa_wk, xa_wv, l)
        h = _xattn(h.reshape(b, s, d), k_mem, v_mem, g[4], g[5], xa_wq, xa_wo, l).reshape(b * s, d)
        h = _ffn(h, g[6], g[7], ffn2_gate, ffn2_up, ffn2_down, l)
    return h.reshape(b, s, d)
```
